```python
import math
import jax
import jax.numpy as jnp
from jax import lax
import numpy as np

D_MODEL = 2048
BATCH = 16
SEQ = 2048
DEPTH = 4

CTX_LEN = 256
GRID_W = 64

SSM_WIDTH = D_MODEL
SSM_HEADDIM = 64
SSM_HEADS = SSM_WIDTH // SSM_HEADDIM
SSM_GROUPS = 8
SSM_STATE = 128
SSM_CHUNK = 128
CONV_WIDTH = 5
XBC_WIDTH = SSM_WIDTH + 2 * SSM_GROUPS * SSM_STATE

GMLP_WIDTH = D_MODEL
GMLP_GROUPS = 8
GMLP_CHUNK = 128

N_EXPERTS = 32
TOP_K = 4
EXPERT_FF = D_MODEL // 4
SWIGLU_LIMIT = 7.0
SWIGLU_ALPHA = 1.702

N_MOD = 6
LN_EPS = 1e-5
DEEPNORM_ALPHA = (2 * DEPTH) ** 0.25
DEEPNORM_BETA = (8 * DEPTH) ** -0.25

OFF_XBC = SSM_WIDTH
OFF_DTF = OFF_XBC + XBC_WIDTH
OFF_DTB = OFF_DTF + SSM_HEADS
OFF_U = OFF_DTB + SSM_HEADS
OFF_V = OFF_U + GMLP_WIDTH
OFF_GA = OFF_V + GMLP_WIDTH
OFF_GB = OFF_GA + D_MODEL
IN_WIDTH = OFF_GB + D_MODEL

kernel_name = 'hybrid_ssd_gmlp_moe_prefix_trunk'


def layer_norm(t, gain=None, bias=None):
    tf = t.astype(jnp.float32)
    mu = jnp.mean(tf, axis=-1, keepdims=True)
    var = jnp.mean(jnp.square(tf - mu), axis=-1, keepdims=True)
    y = (tf - mu) * lax.rsqrt(var + LN_EPS)
    if gain is not None:
        y = y * gain + bias
    return y.astype(t.dtype)


def modulate(t, shift, scale):
    return layer_norm(t) * (1.0 + scale) + shift


def to_column_major(t):
    bsz, s, d = t.shape
    rows = s // GRID_W
    return t.reshape(bsz, rows, GRID_W, d).swapaxes(1, 2).reshape(bsz, s, d)


def to_row_major(t):
    bsz, s, d = t.shape
    rows = s // GRID_W
    return t.reshape(bsz, GRID_W, rows, d).swapaxes(1, 2).reshape(bsz, s, d)


def flip(t):
    return jnp.flip(t, axis=1)


def centred_depthwise_conv(t, w, b):
    y = lax.conv_general_dilated(
        t, w[:, None, :].astype(t.dtype), window_strides=(1,),
        padding=((CONV_WIDTH // 2, CONV_WIDTH // 2),),
        dimension_numbers=('NWC', 'WIO', 'NWC'), feature_group_count=t.shape[-1])
    return y + b.astype(t.dtype)


def ssd_columns(p, base):
    return (p[..., OFF_XBC - base:OFF_DTF - base],
            p[..., OFF_DTF - base:OFF_DTB - base],
            p[..., OFF_DTB - base:OFF_U - base])


def ssd_stream_inputs(xbc, dt_f, dt_b, conv_w, conv_b, dt_bias_f, dt_bias_b):
    bsz, length, _ = xbc.shape
    gn = SSM_GROUPS * SSM_STATE
    xbc = jax.nn.silu(centred_depthwise_conv(xbc, conv_w, conv_b))
    xs = xbc[..., :SSM_WIDTH].reshape(bsz, length, SSM_HEADS, SSM_HEADDIM)
    bm = xbc[..., SSM_WIDTH:SSM_WIDTH + gn].reshape(bsz, length, SSM_GROUPS, SSM_STATE)
    cm = xbc[..., SSM_WIDTH + gn:].reshape(bsz, length, SSM_GROUPS, SSM_STATE)
    dtf = jax.nn.softplus(dt_f.astype(jnp.float32) + dt_bias_f.astype(jnp.float32))
    dtb = jax.nn.softplus(dt_b.astype(jnp.float32) + dt_bias_b.astype(jnp.float32))
    return xs, bm, cm, dtf, dtb


def ssd_chunked(xs, dt, a, bm, cm, h0):
    bsz, length = xs.shape[0], xs.shape[1]
    nc = length // SSM_CHUNK
    hpg = SSM_HEADS // SSM_GROUPS
    f32 = jnp.float32
    xdt = (xs.astype(f32) * dt[..., None]).reshape(bsz, nc, SSM_CHUNK, SSM_GROUPS, hpg, SSM_HEADDIM)
    da = (dt * a).reshape(bsz, nc, SSM_CHUNK, SSM_GROUPS, hpg)
    bq = bm.astype(f32).reshape(bsz, nc, SSM_CHUNK, SSM_GROUPS, SSM_STATE)
    cq = cm.astype(f32).reshape(bsz, nc, SSM_CHUNK, SSM_GROUPS, SSM_STATE)
    cs = jnp.cumsum(da, axis=2)
    seg = cs[:, :, :, None] - cs[:, :, None, :]
    lower = jnp.tril(jnp.ones((SSM_CHUNK, SSM_CHUNK), dtype=bool))[:, :, None, None]
    decay = jnp.exp(jnp.where(lower, seg, -jnp.inf))
    cb = jnp.einsum('bcqgn,bcsgn->bcqsg', cq, bq)
    y_diag = jnp.einsum('bcqsgr,bcsgrp->bcqgrp', cb[..., None] * decay, xdt)
    to_end = jnp.exp(cs[:, :, -1:] - cs)
    chunk_states = jnp.einsum('bcsgn,bcsgrp->bcgrpn', bq, xdt * to_end[..., None])
    chunk_decay = jnp.exp(cs[:, :, -1])

    def step(h, inp):
        st, dec = inp
        return h * dec[..., None, None] + st, h

    h_init = h0.astype(f32).reshape(bsz, SSM_GROUPS, hpg, SSM_HEADDIM, SSM_STATE)
    h_final, h_enter = lax.scan(step, h_init, (jnp.moveaxis(chunk_states, 1, 0), jnp.moveaxis(chunk_decay, 1, 0)))
    h_enter = jnp.moveaxis(h_enter, 0, 1)
    y_off = jnp.einsum('bcqgn,bcgrpn->bcqgrp', cq, h_enter) * jnp.exp(cs)[..., None]
    y = (y_diag + y_off).reshape(bsz, length, SSM_HEADS, SSM_HEADDIM)
    return y.astype(xs.dtype), h_final.reshape(bsz, SSM_HEADS, SSM_HEADDIM, SSM_STATE)


def gated_rmsnorm(y, z, w):
    bsz, length = y.shape[0], y.shape[1]
    g = y.reshape(bsz, length, SSM_WIDTH).astype(jnp.float32) * jax.nn.silu(z.astype(jnp.float32))
    g = g.reshape(bsz, length, SSM_GROUPS, SSM_WIDTH // SSM_GROUPS)
    g = g * lax.rsqrt(jnp.mean(jnp.square(g), axis=-1, keepdims=True) + LN_EPS)
    return (g.reshape(bsz, length, SSM_WIDTH) * w).astype(y.dtype)


def gmlp_branch(u, v, ln_g, ln_b, w_s, b_s):
    bsz, length, _ = u.shape
    v = layer_norm(jax.nn.gelu(v), ln_g, ln_b)
    vc = v.reshape(bsz, length // GMLP_CHUNK, GMLP_CHUNK, GMLP_GROUPS, GMLP_WIDTH // GMLP_GROUPS)
    mixed = jnp.einsum('gqs,bcsgd->bcqgd', w_s, vc) + b_s.T[:, :, None]
    return jax.nn.gelu(u) * mixed.reshape(bsz, length, GMLP_WIDTH)


def merge_branches(p, y_ssd, y_gmlp, w_pa, w_pb, w_o):
    gate_a = jax.nn.sigmoid(p[..., OFF_GA:OFF_GB])
    gate_b = jax.nn.sigmoid(p[..., OFF_GB:IN_WIDTH])
    return (gate_a * (y_ssd @ w_pa) + gate_b * (y_gmlp @ w_pb)) @ w_o


def token_mixer(h_lat, h_ctx, ctx_out, w_in, b_in, conv_w, conv_b, a_log_f, a_log_b,
                dt_bias_f, dt_bias_b, d_skip, ssm_norm_w, gmlp_ln_g, gmlp_ln_b, w_s, b_s,
                w_pa, w_pb, w_o):
    p_lat = h_lat @ w_in + b_in
    if ctx_out:
        p_ctx, base = h_ctx @ w_in + b_in, 0
    else:
        p_ctx, base = h_ctx @ w_in[:, OFF_XBC:OFF_U] + b_in[OFF_XBC:OFF_U], OFF_XBC
    xl, bl, cl, dtfl, dtbl = ssd_stream_inputs(*ssd_columns(p_lat, 0), conv_w, conv_b, dt_bias_f, dt_bias_b)
    xc, bc, cc, dtfc, dtbc = ssd_stream_inputs(*ssd_columns(p_ctx, base), conv_w, conv_b, dt_bias_f, dt_bias_b)
    a_f = -jnp.exp(a_log_f.astype(jnp.float32))
    a_b = -jnp.exp(a_log_b.astype(jnp.float32))
    h0 = jnp.zeros((h_ctx.shape[0], SSM_HEADS, SSM_HEADDIM, SSM_STATE), jnp.float32)
    y_cf, h_cf = ssd_chunked(xc, dtfc, a_f, bc, cc, h0)
    y_cb, h_cb = ssd_chunked(flip(xc), flip(dtbc), a_b, flip(bc), flip(cc), h0)
    y_lf, _ = ssd_chunked(xl, dtfl, a_f, bl, cl, h_cf)
    y_lb, _ = ssd_chunked(flip(xl), flip(dtbl), a_b, flip(bl), flip(cl), h_cb)
    d = d_skip[:, None]
    ssd_lat = gated_rmsnorm(y_lf + flip(y_lb) + d * xl, p_lat[..., :OFF_XBC], ssm_norm_w)
    g_lat = gmlp_branch(p_lat[..., OFF_U:OFF_V], p_lat[..., OFF_V:OFF_GA], gmlp_ln_g, gmlp_ln_b, w_s, b_s)
    out_lat = merge_branches(p_lat, ssd_lat, g_lat, w_pa, w_pb, w_o)
    if not ctx_out:
        return out_lat, None
    ssd_ctx = gated_rmsnorm(y_cf + flip(y_cb) + d * xc, p_ctx[..., :OFF_XBC], ssm_norm_w)
    g_ctx = gmlp_branch(p_ctx[..., OFF_U:OFF_V], p_ctx[..., OFF_V:OFF_GA], gmlp_ln_g, gmlp_ln_b, w_s, b_s)
    out_ctx = merge_branches(p_ctx, ssd_ctx, g_ctx, w_pa, w_pb, w_o)
    return out_lat, out_ctx


def clamped_swiglu(hidden):
    glu, lin = jnp.split(hidden, 2, axis=-1)
    glu = jnp.minimum(glu, SWIGLU_LIMIT)
    lin = jnp.clip(lin, -SWIGLU_LIMIT, SWIGLU_LIMIT)
    return glu * jax.nn.sigmoid(SWIGLU_ALPHA * glu) * (lin + 1.0)


def moe_ffn(h, w_router, b_router, w_up, b_up, w_down, b_down):
    logits = (h @ w_router + b_router).astype(jnp.float32)
    top_logits, top_idx = lax.top_k(logits, TOP_K)
    top_w = jax.nn.softmax(top_logits, axis=-1)
    combine = jnp.sum(jax.nn.one_hot(top_idx, N_EXPERTS, dtype=jnp.float32) * top_w[..., None], axis=1).astype(h.dtype)
    out = jnp.zeros_like(h)
    for e in range(N_EXPERTS):
        out = out + combine[:, e:e + 1] * (clamped_swiglu(h @ w_up[e] + b_up[e]) @ w_down[e] + b_down[e])
    return out


def setup_inputs(seed: int = 0) -> dict:
    key = jax.random.key(seed)
    ks = jax.random.split(key, 32)
    f32 = jnp.float32

    def normal(k, shape, scale):
        return jax.random.normal(k, shape, f32) * scale

    dt_lo, dt_hi = math.log(1e-3), math.log(1e-1)
    dt_f0 = jnp.exp(jax.random.uniform(ks[12], (DEPTH, SSM_HEADS), f32, dt_lo, dt_hi))
    dt_b0 = jnp.exp(jax.random.uniform(ks[13], (DEPTH, SSM_HEADS), f32, dt_lo, dt_hi))
    return {
        'x': normal(ks[0], (BATCH, SEQ, D_MODEL), 1.0),
        'c': normal(ks[1], (BATCH, D_MODEL), 1.0),
        'ctx': normal(ks[2], (BATCH, CTX_LEN, D_MODEL), 1.0),
        'c_ctx': normal(ks[3], (D_MODEL,), 1.0),
        'w_ada': normal(ks[4], (DEPTH, D_MODEL, N_MOD * D_MODEL), 0.5 * D_MODEL ** -0.5),
        'b_ada': normal(ks[5], (DEPTH, N_MOD * D_MODEL), 0.02),
        'w_in': normal(ks[6], (DEPTH, D_MODEL, IN_WIDTH), D_MODEL ** -0.5),
        'b_in': normal(ks[7], (DEPTH, IN_WIDTH), 0.02),
        'conv_w': normal(ks[8], (DEPTH, CONV_WIDTH, XBC_WIDTH), CONV_WIDTH ** -0.5),
        'conv_b': normal(ks[9], (DEPTH, XBC_WIDTH), 0.02),
        'a_log_f': jnp.log(jax.random.uniform(ks[10], (DEPTH, SSM_HEADS), f32, 1.0, 16.0)),
        'a_log_b': jnp.log(jax.random.uniform(ks[11], (DEPTH, SSM_HEADS), f32, 1.0, 16.0)),
        'dt_bias_f': dt_f0 + jnp.log(-jnp.expm1(-dt_f0)),
        'dt_bias_b': dt_b0 + jnp.log(-jnp.expm1(-dt_b0)),
        'd_skip': 1.0 + normal(ks[14], (DEPTH, SSM_HEADS), 0.1),
        'ssm_norm_w': 1.0 + normal(ks[15], (DEPTH, SSM_WIDTH), 0.1),
        'gmlp_ln_g': 1.0 + normal(ks[16], (DEPTH, GMLP_WIDTH), 0.1),
        'gmlp_ln_b': normal(ks[17], (DEPTH, GMLP_WIDTH), 0.02),
        'w_s': normal(ks[18], (DEPTH, GMLP_GROUPS, GMLP_CHUNK, GMLP_CHUNK), GMLP_CHUNK ** -0.5),
        'b_s': 1.0 + normal(ks[19], (DEPTH, GMLP_GROUPS, GMLP_CHUNK), 0.1),
        'w_pa': normal(ks[20], (DEPTH, SSM_WIDTH, D_MODEL), SSM_WIDTH ** -0.5),
        'w_pb': normal(ks[21], (DEPTH, GMLP_WIDTH, D_MODEL), GMLP_WIDTH ** -0.5),
        'w_o': normal(ks[22], (DEPTH, D_MODEL, D_MODEL), DEEPNORM_BETA * D_MODEL ** -0.5),
        'ln_g': 1.0 + normal(ks[23], (DEPTH, 2, D_MODEL), 0.1),
        'ln_b': normal(ks[24], (DEPTH, 2, D_MODEL), 0.02),
        'w_router': normal(ks[25], (DEPTH, D_MODEL, N_EXPERTS), D_MODEL ** -0.5),
        'b_router': normal(ks[26], (DEPTH, N_EXPERTS), 0.01),
        'w_up': normal(ks[27], (DEPTH, N_EXPERTS, D_MODEL, 2 * EXPERT_FF), D_MODEL ** -0.5),
        'b_up': normal(ks[28], (DEPTH, N_EXPERTS, 2 * EXPERT_FF), 0.02),
        'w_down': normal(ks[29], (DEPTH, N_EXPERTS, EXPERT_FF, D_MODEL), DEEPNORM_BETA * EXPERT_FF ** -0.5),
        'b_down': normal(ks[30], (DEPTH, N_EXPERTS, D_MODEL), 0.02),
    }


def reference(x, c, ctx, c_ctx, w_ada, b_ada, w_in, b_in, conv_w, conv_b, a_log_f, a_log_b,
              dt_bias_f, dt_bias_b, d_skip, ssm_norm_w, gmlp_ln_g, gmlp_ln_b, w_s, b_s,
              w_pa, w_pb, w_o, ln_g, ln_b, w_router, b_router, w_up, b_up, w_down, b_down):
    bsz, seq, d = x.shape
    n_lat = bsz * seq
    lat, cx = x, ctx
    c_act = jax.nn.silu(c)
    cctx_act = jax.nn.silu(c_ctx)
    for l in range(DEPTH):
        last = l == DEPTH - 1
        column_major = l % 2 == 1
        mod_lat = (c_act @ w_ada[l] + b_ada[l]).reshape(bsz, N_MOD, 1, d)
        mod_ctx = (cctx_act @ w_ada[l] + b_ada[l]).reshape(N_MOD, d)
        h_lat = modulate(lat, mod_lat[:, 0], mod_lat[:, 1])
        h_ctx = modulate(cx, mod_ctx[0], mod_ctx[1])
        if column_major:
            h_lat = to_column_major(h_lat)
        y_lat, y_ctx = token_mixer(
            h_lat, h_ctx, not last, w_in[l], b_in[l], conv_w[l], conv_b[l], a_log_f[l], a_log_b[l],
            dt_bias_f[l], dt_bias_b[l], d_skip[l], ssm_norm_w[l], gmlp_ln_g[l], gmlp_ln_b[l],
            w_s[l], b_s[l], w_pa[l], w_pb[l], w_o[l])
        if column_major:
            y_lat = to_row_major(y_lat)
        lat = layer_norm(DEEPNORM_ALPHA * lat + mod_lat[:, 2] * y_lat, ln_g[l, 0], ln_b[l, 0])
        tokens = modulate(lat, mod_lat[:, 3], mod_lat[:, 4]).reshape(n_lat, d)
        if not last:
            cx = layer_norm(DEEPNORM_ALPHA * cx + mod_ctx[2] * y_ctx, ln_g[l, 0], ln_b[l, 0])
            h_ctx = modulate(cx, mod_ctx[3], mod_ctx[4])
            tokens = jnp.concatenate([tokens, h_ctx.reshape(-1, d)], axis=0)
        y = moe_ffn(tokens, w_router[l], b_router[l], w_up[l], b_up[l], w_down[l], b_down[l])
        lat = layer_norm(DEEPNORM_ALPHA * lat + mod_lat[:, 5] * y[:n_lat].reshape(bsz, seq, d), ln_g[l, 1], ln_b[l, 1])
        if not last:
            cx = layer_norm(DEEPNORM_ALPHA * cx + mod_ctx[5] * y[n_lat:].reshape(cx.shape), ln_g[l, 1], ln_b[l, 1])
    return lat
```

```python
import functools

import jax
import jax.numpy as jnp
from jax import lax
from jax.experimental import pallas as pl
from jax.experimental.pallas import tpu as pltpu

F32 = jnp.float32
BF16 = jnp.bfloat16
U32 = jnp.uint32
I32 = jnp.int32

SSM_GROUPS = 8
SSM_CHUNK = 128
TOP_K = 4
GRID_W = 64
N_MOD = 6
SWIGLU_LIMIT = 7.0
SWIGLU_ALPHA = 1.702
LN_EPS = 1e-5

LANES = 128
VMEM_LIMIT = 56 * 1024 * 1024
MASK_NEG = -1e30


def _cparams(sem):
    return pltpu.CompilerParams(dimension_semantics=sem, vmem_limit_bytes=VMEM_LIMIT)


def _tile(n, pref, mult=8):
    if n <= pref:
        return n
    t = (pref // mult) * mult
    while t >= mult:
        if n % t == 0:
            return t
        t -= mult
    raise ValueError(f"no tile for {n} <= {pref}")


def _ln(x):
    mu = jnp.mean(x, axis=-1, keepdims=True)
    xc = x - mu
    var = jnp.mean(xc * xc, axis=-1, keepdims=True)
    return xc * lax.rsqrt(var + LN_EPS)


def _sigmoid(x):
    return 1.0 / (1.0 + jnp.exp(-x))


def _gelu_tanh(x):
    c = 0.7978845608028654
    return 0.5 * x * (1.0 + jnp.tanh(c * (x + 0.044715 * (x * x * x))))


def _softplus(x):
    return jnp.maximum(x, 0.0) + jnp.log(1.0 + jnp.exp(-jnp.abs(x)))


def _pack_bf16_pair(y):
    dh = y.shape[-1] // 2
    lo = lax.bitcast_convert_type(y[:, :dh].astype(BF16).astype(F32), U32)
    hi = lax.bitcast_convert_type(y[:, dh:].astype(BF16).astype(F32), U32)
    return (hi & jnp.uint32(0xFFFF0000)) | (lo >> 16)


def _unpack_bf16_pair(w):
    lo = lax.bitcast_convert_type(w << 16, F32)
    hi = lax.bitcast_convert_type(w & jnp.uint32(0xFFFF0000), F32)
    return lo, hi


def _ada_kernel(c_ref, w_ref, b_ref, o_ref):
    c = c_ref[...]
    a = (c * _sigmoid(c)).astype(BF16)
    o_ref[...] = jnp.dot(a, w_ref[...].astype(BF16), preferred_element_type=F32) + b_ref[...]


def _ada(c_all, w_ada, b_ada):
    depth, d, n = w_ada.shape
    r = c_all.shape[0]
    tn = _tile(n, 1024, LANES)
    return pl.pallas_call(
        _ada_kernel,
        out_shape=jax.ShapeDtypeStruct((depth, r, n), F32),
        grid=(depth, n // tn),
        in_specs=[
            pl.BlockSpec((r, d), lambda l, j: (0, 0)),
            pl.BlockSpec((None, d, tn), lambda l, j: (l, 0, j)),
            pl.BlockSpec((None, 1, tn), lambda l, j: (l, 0, j)),
        ],
        out_specs=pl.BlockSpec((None, r, tn), lambda l, j: (l, 0, j)),
        compiler_params=_cparams(("parallel", "parallel")),
        name="ada_mod",
    )(c_all, w_ada, b_ada.reshape(depth, 1, n))


def _ln_mod_kernel(s_ref, mod_ref, h_ref, *, shift_row, scale_row):
    y = _ln(s_ref[...])
    scale = mod_ref[scale_row:scale_row + 1, :]
    shift = mod_ref[shift_row:shift_row + 1, :]
    h_ref[...] = (y * (1.0 + scale) + shift).astype(h_ref.dtype)


def _ln_mod(s_all, mod, ctx_len, shift_row, scale_row):
    b, s, d = s_all.shape
    tr = _tile(ctx_len, 256)
    nct = ctx_len // tr
    return pl.pallas_call(
        functools.partial(_ln_mod_kernel, shift_row=shift_row, scale_row=scale_row),
        out_shape=jax.ShapeDtypeStruct((b, s, d), BF16),
        grid=(b, s // tr),
        in_specs=[
            pl.BlockSpec((None, tr, d), lambda bi, i: (bi, i, 0)),
            pl.BlockSpec((None, None, N_MOD, d), lambda bi, i: (bi, jnp.minimum(i // nct, 1), 0, 0)),
        ],
        out_specs=pl.BlockSpec((None, tr, d), lambda bi, i: (bi, i, 0)),
        compiler_params=_cparams(("parallel", "parallel")),
        name="ln_mod",
    )(s_all, mod)


def _mm_kernel(a_ref, w_ref, b_ref, o_ref):
    acc = jnp.dot(a_ref[...], w_ref[...], preferred_element_type=F32)
    o_ref[...] = (acc + b_ref[...]).astype(o_ref.dtype)


def _matmul(a, w, bias, out_dtype, name):
    m, k = a.shape
    n = w.shape[1]
    tm = _tile(m, 1024)
    tn = _tile(n, 1024, LANES)
    return pl.pallas_call(
        _mm_kernel,
        out_shape=jax.ShapeDtypeStruct((m, n), out_dtype),
        grid=(n // tn, m // tm),
        in_specs=[
            pl.BlockSpec((tm, k), lambda j, i: (i, 0)),
            pl.BlockSpec((k, tn), lambda j, i: (0, j)),
            pl.BlockSpec((1, tn), lambda j, i: (0, j)),
        ],
        out_specs=pl.BlockSpec((tm, tn), lambda j, i: (i, j)),
        compiler_params=_cparams(("parallel", "parallel")),
        name=name,
    )(a, w, bias)


_CONV_PAD = 8


def _conv_kernel(x_ref, w_ref, b_ref, o_ref, xs_ref, *, segs, kw, rc):
    half = kw // 2
    cw = x_ref.shape[-1]
    zeros = jnp.zeros((_CONV_PAD, cw), F32)
    off = 0
    bases = []
    for st, ln in segs:
        xs_ref[off:off + _CONV_PAD, :] = zeros
        xs_ref[off + _CONV_PAD:off + _CONV_PAD + ln, :] = x_ref[st:st + ln, :].astype(F32)
        bases.append(off + _CONV_PAD)
        off += _CONV_PAD + ln
    xs_ref[off:off + _CONV_PAD, :] = zeros
    for (st, ln), base in zip(segs, bases):
        for r0 in range(0, ln, rc):
            acc = b_ref[...] + w_ref[0:1, :] * xs_ref[base + r0 - half:base + r0 - half + rc, :]
            for k in range(1, kw):
                lo = base + r0 + k - half
                acc = acc + w_ref[k:k + 1, :] * xs_ref[lo:lo + rc, :]
            o_ref[st + r0:st + r0 + rc, :] = (acc * _sigmoid(acc)).astype(o_ref.dtype)


def _conv_silu(p_main, conv_w, conv_b, col_off, segs):
    b, s, _ = p_main.shape
    kw, xbc = conv_w.shape
    cw = 512 if (xbc % 512 == 0 and col_off % 512 == 0) else LANES
    assert xbc % cw == 0 and col_off % cw == 0 and kw // 2 <= _CONV_PAD
    rc = _tile(min(ln for _, ln in segs), 256)
    assert all(ln % rc == 0 for _, ln in segs)
    rows = s + _CONV_PAD * (len(segs) + 1)
    ob = col_off // cw
    return pl.pallas_call(
        functools.partial(_conv_kernel, segs=segs, kw=kw, rc=rc),
        out_shape=jax.ShapeDtypeStruct((b, s, xbc), BF16),
        grid=(b, xbc // cw),
        in_specs=[
            pl.BlockSpec((None, s, cw), lambda bi, j: (bi, 0, ob + j)),
            pl.BlockSpec((kw, cw), lambda bi, j: (0, j)),
            pl.BlockSpec((1, cw), lambda bi, j: (0, j)),
        ],
        out_specs=pl.BlockSpec((None, s, cw), lambda bi, j: (bi, 0, j)),
        scratch_shapes=[pltpu.VMEM((rows, cw), F32)],
        compiler_params=_cparams(("parallel", "parallel")),
        name="conv_silu",
    )(p_main, conv_w, conv_b.reshape(1, xbc))


def _expand_cols(v, c0, hpg, p):
    r = v.shape[0]
    width = hpg * p
    lane_head = lax.broadcasted_iota(I32, (1, width), 1) // p
    out = jnp.broadcast_to(v[:, c0:c0 + 1], (r, width))
    for h in range(1, hpg):
        out = jnp.where(lane_head == h, jnp.broadcast_to(v[:, c0 + h:c0 + h + 1], (r, width)), out)
    return out


def _ssd_dir(xa_ref, dtr_ref, arow, dtb_row, y_ref, st_ref, *, backward, nh, ng, p, n, q):
    hpg = nh // ng
    width = nh * p
    gw = hpg * p
    col0 = nh if backward else 0
    dt = _softplus(dtr_ref[...] + dtb_row)
    da = dt * arow
    qi = lax.broadcasted_iota(I32, (q, q), 0)
    si = lax.broadcasted_iota(I32, (q, q), 1)
    mask = (qi <= si) if backward else (qi >= si)
    tri = jnp.where(mask, 1.0, 0.0).astype(F32)
    cs = jnp.dot(tri, da, preferred_element_type=F32, precision=lax.Precision.HIGHEST)
    tot = cs[0:1, :] if backward else cs[q - 1:q, :]
    wts = dt * jnp.exp(tot - cs)
    ecs = jnp.exp(cs)
    cdec = jnp.exp(tot)
    cs_t = cs.T
    lane_head = lax.broadcasted_iota(I32, (1, gw), 1) // p
    for g in range(ng):
        xg = xa_ref[:, g * gw:(g + 1) * gw].astype(F32)
        bg = xa_ref[:, width + g * n:width + (g + 1) * n]
        cg = xa_ref[:, width + ng * n + g * n:width + ng * n + (g + 1) * n]
        cb = lax.dot_general(cg, bg, (((1,), (1,)), ((), ())), preferred_element_type=F32)
        c0 = col0 + g * hpg
        ms = []
        for h in range(hpg):
            c = c0 + h
            seg = cs[:, c:c + 1] - cs_t[c:c + 1, :]
            dec = jnp.exp(jnp.where(mask, seg, MASK_NEG))
            ms.append((cb * dec).astype(BF16))
        mcat = jnp.concatenate(ms, axis=1)
        xdt = xg * _expand_cols(dt, c0, hpg, p)
        xbd = jnp.concatenate(
            [jnp.where(lane_head == h, xdt, 0.0).astype(BF16) for h in range(hpg)], axis=0)
        ydiag = jnp.dot(mcat, xbd, preferred_element_type=F32)
        xw = (xg * _expand_cols(wts, c0, hpg, p)).astype(BF16)
        st_new = lax.dot_general(bg, xw, (((0,), (0,)), ((), ())), preferred_element_type=F32)
        st_old = st_ref[g]
        yoff = jnp.dot(cg, st_old.astype(BF16), preferred_element_type=F32)
        yoff = yoff * _expand_cols(ecs, c0, hpg, p)
        st_ref[g] = st_old * _expand_cols(cdec, c0, hpg, p) + st_new
        y_ref[:, g * gw:(g + 1) * gw] = (ydiag + yoff).astype(y_ref.dtype)


def _ssd_kernel(xf_ref, xb_ref, dtf_ref, dtb_ref, alog_ref, dtbias_ref, yf_ref, yb_ref,
                sf_ref, sb_ref, **kw):
    @pl.when(pl.program_id(1) == 0)
    def _():
        sf_ref[...] = jnp.zeros_like(sf_ref)
        sb_ref[...] = jnp.zeros_like(sb_ref)

    arow = -jnp.exp(alog_ref[...])
    dtb_row = dtbias_ref[...]
    _ssd_dir(xf_ref, dtf_ref, arow, dtb_row, yf_ref, sf_ref, backward=False, **kw)
    _ssd_dir(xb_ref, dtb_ref, arow, dtb_row, yb_ref, sb_ref, backward=True, **kw)


def _ssd(xact, dtraw, alog_row, dtbias_row, nh, p, n, ctx_len):
    b, s, xbc = xact.shape
    q = SSM_CHUNK
    ng = SSM_GROUPS
    width = nh * p
    ncc = ctx_len // q
    nc = s // q
    ncl = nc - ncc

    def fwd(bi, i):
        return (bi, i, 0)

    def bwd(bi, i):
        return (bi, jnp.where(i < ncc, ncc - 1 - i, 2 * ncc + ncl - 1 - i), 0)

    kern = functools.partial(_ssd_kernel, nh=nh, ng=ng, p=p, n=n, q=q)
    return pl.pallas_call(
        kern,
        out_shape=(jax.ShapeDtypeStruct((b, s, width), BF16), jax.ShapeDtypeStruct((b, s, width), BF16)),
        grid=(b, nc),
        in_specs=[
            pl.BlockSpec((None, q, xbc), fwd),
            pl.BlockSpec((None, q, xbc), bwd),
            pl.BlockSpec((None, q, LANES), fwd),
            pl.BlockSpec((None, q, LANES), bwd),
            pl.BlockSpec((1, LANES), lambda bi, i: (0, 0)),
            pl.BlockSpec((1, LANES), lambda bi, i: (0, 0)),
        ],
        out_specs=(pl.BlockSpec((None, q, width), fwd), pl.BlockSpec((None, q, width), bwd)),
        scratch_shapes=[pltpu.VMEM((ng, n, (nh // ng) * p), F32), pltpu.VMEM((ng, n, (nh // ng) * p), F32)],
        compiler_params=_cparams(("parallel", "arbitrary")),
        name="ssd_scan",
    )(xact, xact, dtraw, dtraw, alog_row, dtbias_row)


def _gnorm_kernel(yf_ref, yb_ref, x_ref, z_ref, d_ref, w_ref, o_ref, *, ng):
    x = x_ref[...].astype(F32)
    z = z_ref[...].astype(F32)
    y = yf_ref[...].astype(F32) + yb_ref[...].astype(F32) + d_ref[...] * x
    g = y * (z * _sigmoid(z))
    gw = g.shape[-1] // ng
    for k in range(ng):
        gk = g[:, k * gw:(k + 1) * gw]
        ms = jnp.mean(gk * gk, axis=-1, keepdims=True)
        o_ref[:, k * gw:(k + 1) * gw] = (
            gk * lax.rsqrt(ms + LN_EPS) * w_ref[:, k * gw:(k + 1) * gw]).astype(o_ref.dtype)


def _gated_rmsnorm(yf, yb, xact, p_main, d_exp, norm_w):
    b, s, width = yf.shape
    tr = _tile(s, 256)
    blk = pl.BlockSpec((None, tr, width), lambda bi, i: (bi, i, 0))
    row = pl.BlockSpec((1, width), lambda bi, i: (0, 0))
    return pl.pallas_call(
        functools.partial(_gnorm_kernel, ng=SSM_GROUPS),
        out_shape=jax.ShapeDtypeStruct((b, s, width), BF16),
        grid=(b, s // tr),
        in_specs=[blk, blk, blk, blk, row, row],
        out_specs=blk,
        compiler_params=_cparams(("parallel", "parallel")),
        name="gated_rmsnorm",
    )(yf, yb, xact, p_main, d_exp, norm_w)


def _gmlp_kernel(u_ref, v_ref, lng_ref, lnb_ref, ws_ref, bst_ref, o_ref, *, ng):
    v = _gelu_tanh(v_ref[...].astype(F32))
    vb = (_ln(v) * lng_ref[...] + lnb_ref[...]).astype(BF16)
    u = _gelu_tanh(u_ref[...].astype(F32))
    gw = u.shape[-1] // ng
    for g in range(ng):
        mixed = jnp.dot(ws_ref[g], vb[:, g * gw:(g + 1) * gw], preferred_element_type=F32)
        mixed = mixed + bst_ref[:, g:g + 1]
        o_ref[:, g * gw:(g + 1) * gw] = (u[:, g * gw:(g + 1) * gw] * mixed).astype(o_ref.dtype)


def _gmlp(p_main, off_u, off_v, ln_g, ln_b, w_s, b_s):
    b, s, _ = p_main.shape
    ng, q, _ = w_s.shape
    wg = ln_g.shape[-1]
    assert off_u % wg == 0 and off_v % wg == 0
    bu, bv = off_u // wg, off_v // wg
    row = pl.BlockSpec((1, wg), lambda bi, c: (0, 0))
    return pl.pallas_call(
        functools.partial(_gmlp_kernel, ng=ng),
        out_shape=jax.ShapeDtypeStruct((b, s, wg), BF16),
        grid=(b, s // q),
        in_specs=[
            pl.BlockSpec((None, q, wg), lambda bi, c: (bi, c, bu)),
            pl.BlockSpec((None, q, wg), lambda bi, c: (bi, c, bv)),
            row, row,
            pl.BlockSpec((ng, q, q), lambda bi, c: (0, 0, 0)),
            pl.BlockSpec((q, ng), lambda bi, c: (0, 0)),
        ],
        out_specs=pl.BlockSpec((None, q, wg), lambda bi, c: (bi, c, 0)),
        compiler_params=_cparams(("parallel", "parallel")),
        name="gmlp",
    )(p_main, p_main, ln_g.reshape(1, wg), ln_b.reshape(1, wg), w_s.astype(BF16), b_s.T)


def _merge_kernel(a_ref, g_ref, ga_ref, gb_ref, wpa_ref, wpb_ref, o_ref):
    pa = jnp.dot(a_ref[...], wpa_ref[...], preferred_element_type=F32)
    pb = jnp.dot(g_ref[...], wpb_ref[...], preferred_element_type=F32)
    m = _sigmoid(ga_ref[...].astype(F32)) * pa + _sigmoid(gb_ref[...].astype(F32)) * pb
    o_ref[...] = m.astype(o_ref.dtype)


def _merge(ssd_out, g_out, p_main2d, off_ga, off_gb, w_pa, w_pb):
    m, wa = ssd_out.shape
    wb = g_out.shape[1]
    n = w_pa.shape[1]
    tm = _tile(m, 512)
    tn = _tile(n, 1024, LANES)
    assert off_ga % tn == 0 and off_gb % tn == 0
    ba, bb = off_ga // tn, off_gb // tn
    return pl.pallas_call(
        _merge_kernel,
        out_shape=jax.ShapeDtypeStruct((m, n), BF16),
        grid=(n // tn, m // tm),
        in_specs=[
            pl.BlockSpec((tm, wa), lambda j, i: (i, 0)),
            pl.BlockSpec((tm, wb), lambda j, i: (i, 0)),
            pl.BlockSpec((tm, tn), lambda j, i: (i, ba + j)),
            pl.BlockSpec((tm, tn), lambda j, i: (i, bb + j)),
            pl.BlockSpec((wa, tn), lambda j, i: (0, j)),
            pl.BlockSpec((wb, tn), lambda j, i: (0, j)),
        ],
        out_specs=pl.BlockSpec((tm, tn), lambda j, i: (i, j)),
        compiler_params=_cparams(("parallel", "parallel")),
        name="merge_branches",
    )(ssd_out, g_out, p_main2d, p_main2d, w_pa, w_pb)


def _top_k(logits, k):
    lane = lax.broadcasted_iota(I32, logits.shape, 1)
    vals, idxs = [], []
    cur = logits
    for _ in range(k):
        m = jnp.max(cur, axis=-1, keepdims=True)
        idx = jnp.min(jnp.where(cur == m, lane, LANES), axis=-1, keepdims=True)
        vals.append(m)
        idxs.append(idx)
        cur = jnp.where(lane == idx, -jnp.inf, cur)
    return vals, idxs


def _wo_res_kernel(m_ref, wo_ref, s_ref, mod_ref, lng_ref, lnb_ref, wr_ref, br_ref,
                   snew_ref, h_ref, idx_ref, wts_ref, *, alpha, k):
    y = jnp.dot(m_ref[...], wo_ref[...], preferred_element_type=F32)
    t = alpha * s_ref[...] + mod_ref[2:3, :] * y
    sn = _ln(t) * lng_ref[...] + lnb_ref[...]
    snew_ref[...] = sn
    h = _ln(sn) * (1.0 + mod_ref[4:5, :]) + mod_ref[3:4, :]
    h_ref[...] = _pack_bf16_pair(h)
    logits = jnp.dot(h.astype(BF16), wr_ref[...], preferred_element_type=F32) + br_ref[...]
    vals, idxs = _top_k(logits, k)
    es = [jnp.exp(v - vals[0]) for v in vals]
    den = es[0]
    for e in es[1:]:
        den = den + e
    for j in range(k):
        idx_ref[:, j:j + 1] = idxs[j]
        wts_ref[:, j:j + 1] = es[j] / den


def _wo_residual(m_all, w_o, s_all, mod, ln_g, ln_b, w_router, b_router, ctx_len, alpha):
    b, s, d = s_all.shape
    tr = _tile(ctx_len, 256)
    nct = ctx_len // tr
    blk = lambda w: pl.BlockSpec((None, tr, w), lambda bi, i: (bi, i, 0))
    row = pl.BlockSpec((1, d), lambda bi, i: (0, 0))
    return pl.pallas_call(
        functools.partial(_wo_res_kernel, alpha=alpha, k=TOP_K),
        out_shape=(
            jax.ShapeDtypeStruct((b, s, d), F32),
            jax.ShapeDtypeStruct((b, s, d // 2), U32),
            jax.ShapeDtypeStruct((b, s, TOP_K), I32),
            jax.ShapeDtypeStruct((b, s, TOP_K), F32),
        ),
        grid=(b, s // tr),
        in_specs=[
            blk(d),
            pl.BlockSpec((d, d), lambda bi, i: (0, 0)),
            blk(d),
            pl.BlockSpec((None, None, N_MOD, d), lambda bi, i: (bi, jnp.minimum(i // nct, 1), 0, 0)),
            row, row,
            pl.BlockSpec((d, LANES), lambda bi, i: (0, 0)),
            pl.BlockSpec((1, LANES), lambda bi, i: (0, 0)),
        ],
        out_specs=(blk(d), blk(d // 2), blk(TOP_K), blk(TOP_K)),
        compiler_params=_cparams(("parallel", "parallel")),
        name="wo_residual_router",
    )(m_all, w_o, s_all, mod, ln_g, ln_b, w_router, b_router)


_ROW_BLOCK = 8


def _wait_rows(src, dst, sem, nrows):
    nblk = nrows // _ROW_BLOCK

    def wait_block(i, c):
        pltpu.make_async_copy(src.at[pl.ds(0, _ROW_BLOCK)], dst.at[pl.ds(0, _ROW_BLOCK)], sem).wait()
        return c

    def wait_row(i, c):
        pltpu.make_async_copy(src.at[pl.ds(0, 1)], dst.at[pl.ds(0, 1)], sem).wait()
        return c

    lax.fori_loop(0, nblk, wait_block, 0)
    lax.fori_loop(0, nrows - nblk * _ROW_BLOCK, wait_row, 0)


def _expert_kernel(te_ref, nv_ref, src_ref, dst_ref, h_hbm, wup_ref, bup_ref, wdn_ref, bdn_ref,
                   ys_hbm, xbuf, ybuf, gsem, ssem):
    t = pl.program_id(0)
    nv = nv_ref[t]

    @pl.when(nv > 0)
    def _():
        def gather_row(r, c):
            pltpu.make_async_copy(h_hbm.at[pl.ds(src_ref[0, r], 1)], xbuf.at[pl.ds(r, 1)], gsem).start()
            return c

        lax.fori_loop(0, nv, gather_row, 0)
        _wait_rows(h_hbm, xbuf, gsem, nv)

        dh = xbuf.shape[-1]
        lo, hi = _unpack_bf16_pair(xbuf[...])
        hid = jnp.dot(lo.astype(BF16), wup_ref[0:dh, :], preferred_element_type=F32)
        hid = hid + jnp.dot(hi.astype(BF16), wup_ref[dh:, :], preferred_element_type=F32)
        hid = hid + bup_ref[...]
        f = hid.shape[-1] // 2
        glu = jnp.minimum(hid[:, :f], SWIGLU_LIMIT)
        lin = jnp.clip(hid[:, f:], -SWIGLU_LIMIT, SWIGLU_LIMIT)
        act = (glu * _sigmoid(SWIGLU_ALPHA * glu) * (lin + 1.0)).astype(BF16)
        y = jnp.dot(act, wdn_ref[...], preferred_element_type=F32) + bdn_ref[...]
        ybuf[...] = _pack_bf16_pair(y)

        def scatter_row(r, c):
            pltpu.make_async_copy(ybuf.at[pl.ds(r, 1)], ys_hbm.at[pl.ds(dst_ref[0, r], 1)], ssem).start()
            return c

        lax.fori_loop(0, nv, scatter_row, 0)
        _wait_rows(ybuf, ys_hbm, ssem, nv)


def _experts(h_packed, plan, w_up, b_up, w_down, b_down, tm):
    te, nv, src, dst = plan
    m, dh = h_packed.shape
    ne, d, f2 = w_up.shape
    f = f2 // 2
    nt = te.shape[0]
    na = m * TOP_K
    grid_spec = pltpu.PrefetchScalarGridSpec(
        num_scalar_prefetch=2,
        grid=(nt,),
        in_specs=[
            pl.BlockSpec((None, 1, tm), lambda t, te, nv: (t, 0, 0), memory_space=pltpu.SMEM),
            pl.BlockSpec((None, 1, tm), lambda t, te, nv: (t, 0, 0), memory_space=pltpu.SMEM),
            pl.BlockSpec(memory_space=pl.ANY),
            pl.BlockSpec((None, d, f2), lambda t, te, nv: (te[t], 0, 0)),
            pl.BlockSpec((None, 1, f2), lambda t, te, nv: (te[t], 0, 0)),
            pl.BlockSpec((None, f, d), lambda t, te, nv: (te[t], 0, 0)),
            pl.BlockSpec((None, 1, d), lambda t, te, nv: (te[t], 0, 0)),
        ],
        out_specs=pl.BlockSpec(memory_space=pl.ANY),
        scratch_shapes=[
            pltpu.VMEM((tm, dh), U32),
            pltpu.VMEM((tm, dh), U32),
            pltpu.SemaphoreType.DMA(()),
            pltpu.SemaphoreType.DMA(()),
        ],
    )
    return pl.pallas_call(
        _expert_kernel,
        out_shape=jax.ShapeDtypeStruct((na, dh), U32),
        grid_spec=grid_spec,
        compiler_params=_cparams(("arbitrary",)),
        name="moe_experts",
    )(te, nv, src, dst, h_packed, w_up, b_up.reshape(ne, 1, f2), w_down, b_down.reshape(ne, 1, d))


def _moe_plan(idx, ne, tm):
    m, k = idx.shape
    na = m * k
    assert na % tm == 0
    e = idx.reshape(na)
    onehot = (e[:, None] == jnp.arange(ne, dtype=I32)[None, :]).astype(I32)
    csum = jnp.cumsum(onehot, axis=0)
    rank = jnp.sum(csum * onehot, axis=1) - 1
    counts = csum[-1]
    ptiles = (counts + tm - 1) // tm
    tile_end = jnp.cumsum(ptiles)
    tile_start = tile_end - ptiles
    pos = tile_start[e] * tm + rank
    nt = na // tm + ne
    a = jnp.arange(na, dtype=I32)
    src = jnp.zeros((nt * tm,), I32).at[pos].set(a // k)
    dst = jnp.zeros((nt * tm,), I32).at[pos].set(a)
    tids = jnp.arange(nt, dtype=I32)
    te = jnp.minimum(jnp.searchsorted(tile_end, tids, side="right").astype(I32), ne - 1)
    nv = jnp.clip(counts[te] - (tids - tile_start[te]) * tm, 0, tm)
    nv = jnp.where(tids < tile_end[-1], nv, 0).astype(I32)
    return te, nv, src.reshape(nt, 1, tm), dst.reshape(nt, 1, tm)


def _combine_kernel(ys_ref, wts_ref, s_ref, mod_ref, lng_ref, lnb_ref, snew_ref, *, alpha, k):
    dh = ys_ref.shape[-1] // k
    acc = None
    for j in range(k):
        lo, hi = _unpack_bf16_pair(ys_ref[:, j * dh:(j + 1) * dh])
        yk = wts_ref[:, j:j + 1] * jnp.concatenate([lo, hi], axis=1)
        acc = yk if acc is None else acc + yk
    t = alpha * s_ref[...] + mod_ref[5:6, :] * acc
    snew_ref[...] = _ln(t) * lng_ref[...] + lnb_ref[...]


def _combine(ys, wts, s_all, mod, ln_g, ln_b, ctx_len, alpha):
    b, s, d = s_all.shape
    tr = _tile(ctx_len, 256)
    nct = ctx_len // tr
    blk = lambda w: pl.BlockSpec((None, tr, w), lambda bi, i: (bi, i, 0))
    row = pl.BlockSpec((1, d), lambda bi, i: (0, 0))
    return pl.pallas_call(
        functools.partial(_combine_kernel, alpha=alpha, k=TOP_K),
        out_shape=jax.ShapeDtypeStruct((b, s, d), F32),
        grid=(b, s // tr),
        in_specs=[
            blk(TOP_K * (d // 2)),
            blk(TOP_K),
            blk(d),
            pl.BlockSpec((None, None, N_MOD, d), lambda bi, i: (bi, jnp.minimum(i // nct, 1), 0, 0)),
            row, row,
        ],
        out_specs=blk(d),
        compiler_params=_cparams(("parallel", "parallel")),
        name="moe_combine_residual",
    )(ys.reshape(b, s, TOP_K * (d // 2)), wts, s_all, mod, ln_g, ln_b)


def _to_column_major(t, ctx_len):
    b, s, d = t.shape
    lat = t[:, ctx_len:]
    rows = (s - ctx_len) // GRID_W
    lat = lat.reshape(b, rows, GRID_W, d).swapaxes(1, 2).reshape(b, s - ctx_len, d)
    return jnp.concatenate([t[:, :ctx_len], lat], axis=1)


def _to_row_major(t, ctx_len):
    b, s, d = t.shape
    lat = t[:, ctx_len:]
    rows = (s - ctx_len) // GRID_W
    lat = lat.reshape(b, GRID_W, rows, d).swapaxes(1, 2).reshape(b, s - ctx_len, d)
    return jnp.concatenate([t[:, :ctx_len], lat], axis=1)


def kernel(x, c, ctx, c_ctx, w_ada, b_ada, w_in, b_in, conv_w, conv_b, a_log_f, a_log_b, dt_bias_f, dt_bias_b, d_skip, ssm_norm_w, gmlp_ln_g, gmlp_ln_b, w_s, b_s, w_pa, w_pb, w_o, ln_g, ln_b, w_router, b_router, w_up, b_up, w_down, b_down):
    bsz, seq, d = x.shape
    ctx_len = ctx.shape[1]
    depth = w_ada.shape[0]
    s_tot = ctx_len + seq
    m_tot = bsz * s_tot
    nh = a_log_f.shape[1]
    ssm_w = ssm_norm_w.shape[1]
    p = ssm_w // nh
    xbc = conv_w.shape[2]
    n_state = (xbc - ssm_w) // (2 * SSM_GROUPS)
    gw = gmlp_ln_g.shape[1]
    ne = w_router.shape[2]
    assert 2 * nh <= LANES and ne <= LANES

    alpha = (2 * depth) ** 0.25
    off_xbc = ssm_w
    off_dtf = off_xbc + xbc
    off_u = off_dtf + 2 * nh
    off_v = off_u + gw
    off_ga = off_v + gw
    off_gb = off_ga + d
    m_off_u = off_dtf
    m_off_v = m_off_u + gw
    m_off_ga = m_off_v + gw
    m_off_gb = m_off_ga + d

    segs = ((0, ctx_len), (ctx_len, seq))
    tm_moe = 512 if (m_tot * TOP_K) % 512 == 0 else 128

    c_all = jnp.concatenate([c, c_ctx[None, :]], axis=0)
    mods = _ada(c_all, w_ada, b_ada).reshape(depth, bsz + 1, N_MOD, d)

    s_all = jnp.concatenate([ctx, x], axis=1)

    for l in range(depth):
        column_major = l % 2 == 1
        mod = jnp.stack(
            [jnp.broadcast_to(mods[l, bsz][None], (bsz, N_MOD, d)), mods[l, :bsz]], axis=1)

        w_main = jnp.concatenate([w_in[l][:, :off_dtf], w_in[l][:, off_u:]], axis=1).astype(BF16)
        b_main = jnp.concatenate([b_in[l][:off_dtf], b_in[l][off_u:]])[None, :]
        w_dt = jnp.pad(w_in[l][:, off_dtf:off_u], ((0, 0), (0, LANES - 2 * nh))).astype(BF16)
        b_dt = jnp.pad(b_in[l][off_dtf:off_u], (0, LANES - 2 * nh))[None, :]

        h = _ln_mod(s_all, mod, ctx_len, 0, 1)
        if column_major:
            h = _to_column_major(h, ctx_len)
        h2 = h.reshape(m_tot, d)
        p_main = _matmul(h2, w_main, b_main, BF16, "in_proj")
        dtraw = _matmul(h2, w_dt, b_dt, F32, "dt_proj").reshape(bsz, s_tot, LANES)
        p3 = p_main.reshape(bsz, s_tot, -1)

        xact = _conv_silu(p3, conv_w[l], conv_b[l], off_xbc, segs)
        alog_row = jnp.pad(jnp.concatenate([a_log_f[l], a_log_b[l]]), (0, LANES - 2 * nh))[None, :]
        dtb_row = jnp.pad(jnp.concatenate([dt_bias_f[l], dt_bias_b[l]]), (0, LANES - 2 * nh))[None, :]
        yf, yb = _ssd(xact, dtraw, alog_row, dtb_row, nh, p, n_state, ctx_len)
        d_exp = jnp.repeat(d_skip[l], p)[None, :]
        ssd_out = _gated_rmsnorm(yf, yb, xact, p3, d_exp, ssm_norm_w[l][None, :])
        g_out = _gmlp(p3, m_off_u, m_off_v, gmlp_ln_g[l], gmlp_ln_b[l], w_s[l], b_s[l])
        merged = _merge(ssd_out.reshape(m_tot, ssm_w), g_out.reshape(m_tot, gw), p_main,
                        m_off_ga, m_off_gb, w_pa[l].astype(BF16), w_pb[l].astype(BF16))
        merged = merged.reshape(bsz, s_tot, d)
        if column_major:
            merged = _to_row_major(merged, ctx_len)

        w_r = jnp.pad(w_router[l], ((0, 0), (0, LANES - ne))).astype(BF16)
        b_r = jnp.pad(b_router[l], (0, LANES - ne), constant_values=MASK_NEG)[None, :]
        s_all, h_moe, idx, wts = _wo_residual(
            merged, w_o[l].astype(BF16), s_all, mod, ln_g[l, 0][None, :], ln_b[l, 0][None, :],
            w_r, b_r, ctx_len, alpha)

        plan = _moe_plan(idx.reshape(m_tot, TOP_K), ne, tm_moe)
        ys = _experts(h_moe.reshape(m_tot, d // 2), plan, w_up[l].astype(BF16), b_up[l],
                      w_down[l].astype(BF16), b_down[l], tm_moe)
        s_all = _combine(ys, wts, s_all, mod, ln_g[l, 1][None, :], ln_b[l, 1][None, :], ctx_len, alpha)

    return s_all[:, ctx_len:]
```

```python
import functools

import jax
import jax.numpy as jnp
from jax import lax
from jax.experimental import pallas as pl
from jax.experimental.pallas import tpu as pltpu

F32 = jnp.float32
BF16 = jnp.bfloat16
U32 = jnp.uint32
I32 = jnp.int32

SSM_GROUPS = 8
SSM_CHUNK = 128
TOP_K = 4
GRID_W = 64
N_MOD = 6
SWIGLU_LIMIT = 7.0
SWIGLU_ALPHA = 1.702
LN_EPS = 1e-5

LANES = 128
VMEM_LIMIT = 56 * 1024 * 1024
MASK_NEG = -1e30


def _cparams(sem):
    return pltpu.CompilerParams(dimension_semantics=sem, vmem_limit_bytes=VMEM_LIMIT)


def _tile(n, pref, mult=8):
    if n <= pref:
        return n
    t = (pref // mult) * mult
    while t >= mult:
        if n % t == 0:
            return t
        t -= mult
    raise ValueError(f"no tile for {n} <= {pref}")


def _ln(x):
    mu = jnp.mean(x, axis=-1, keepdims=True)
    xc = x - mu
    var = jnp.mean(xc * xc, axis=-1, keepdims=True)
    return xc * lax.rsqrt(var + LN_EPS)


def _sigmoid(x):
    return 1.0 / (1.0 + jnp.exp(-x))


def _gelu_tanh(x):
    c = 0.7978845608028654
    return 0.5 * x * (1.0 + jnp.tanh(c * (x + 0.044715 * (x * x * x))))


def _softplus(x):
    return jnp.maximum(x, 0.0) + jnp.log(1.0 + jnp.exp(-jnp.abs(x)))


def _pack_bf16_pair(y):
    dh = y.shape[-1] // 2
    lo = lax.bitcast_convert_type(y[:, :dh].astype(BF16).astype(F32), U32)
    hi = lax.bitcast_convert_type(y[:, dh:].astype(BF16).astype(F32), U32)
    return (hi & jnp.uint32(0xFFFF0000)) | (lo >> 16)


def _unpack_bf16_pair(w):
    lo = lax.bitcast_convert_type(w << 16, F32)
    hi = lax.bitcast_convert_type(w & jnp.uint32(0xFFFF0000), F32)
    return lo, hi


def _ada_kernel(c_ref, w_ref, b_ref, o_ref):
    c = c_ref[...]
    a = (c * _sigmoid(c)).astype(BF16)
    o_ref[...] = jnp.dot(a, w_ref[...].astype(BF16), preferred_element_type=F32) + b_ref[...]


def _ada(c_all, w_ada, b_ada):
    depth, d, n = w_ada.shape
    r = c_all.shape[0]
    tn = _tile(n, 1024, LANES)
    return pl.pallas_call(
        _ada_kernel,
        out_shape=jax.ShapeDtypeStruct((depth, r, n), F32),
        grid=(depth, n // tn),
        in_specs=[
            pl.BlockSpec((r, d), lambda l, j: (0, 0)),
            pl.BlockSpec((None, d, tn), lambda l, j: (l, 0, j)),
            pl.BlockSpec((None, 1, tn), lambda l, j: (l, 0, j)),
        ],
        out_specs=pl.BlockSpec((None, r, tn), lambda l, j: (l, 0, j)),
        compiler_params=_cparams(("parallel", "parallel")),
        name="ada_mod",
    )(c_all, w_ada, b_ada.reshape(depth, 1, n))


def _ln_mod_kernel(s_ref, mod_ref, h_ref, *, shift_row, scale_row):
    y = _ln(s_ref[...])
    scale = mod_ref[scale_row:scale_row + 1, :]
    shift = mod_ref[shift_row:shift_row + 1, :]
    h_ref[...] = (y * (1.0 + scale) + shift).astype(h_ref.dtype)


def _ln_mod(s_all, mod, ctx_len, shift_row, scale_row):
    b, s, d = s_all.shape
    tr = _tile(ctx_len, 256)
    nct = ctx_len // tr
    return pl.pallas_call(
        functools.partial(_ln_mod_kernel, shift_row=shift_row, scale_row=scale_row),
        out_shape=jax.ShapeDtypeStruct((b, s, d), BF16),
        grid=(b, s // tr),
        in_specs=[
            pl.BlockSpec((None, tr, d), lambda bi, i: (bi, i, 0)),
            pl.BlockSpec((None, None, N_MOD, d), lambda bi, i: (bi, jnp.minimum(i // nct, 1), 0, 0)),
        ],
        out_specs=pl.BlockSpec((None, tr, d), lambda bi, i: (bi, i, 0)),
        compiler_params=_cparams(("parallel", "parallel")),
        name="ln_mod",
    )(s_all, mod)


def _mm_kernel(a_ref, w_ref, b_ref, o_ref):
    acc = jnp.dot(a_ref[...], w_ref[...], preferred_element_type=F32)
    o_ref[...] = (acc + b_ref[...]).astype(o_ref.dtype)


def _matmul(a, w, bias, out_dtype, name):
    m, k = a.shape
    n = w.shape[1]
    tm = _tile(m, 1024)
    tn = _tile(n, 1024, LANES)
    return pl.pallas_call(
        _mm_kernel,
        out_shape=jax.ShapeDtypeStruct((m, n), out_dtype),
        grid=(n // tn, m // tm),
        in_specs=[
            pl.BlockSpec((tm, k), lambda j, i: (i, 0)),
            pl.BlockSpec((k, tn), lambda j, i: (0, j)),
            pl.BlockSpec((1, tn), lambda j, i: (0, j)),
        ],
        out_specs=pl.BlockSpec((tm, tn), lambda j, i: (i, j)),
        compiler_params=_cparams(("parallel", "parallel")),
        name=name,
    )(a, w, bias)


_CONV_PAD = 8


def _conv_kernel(x_ref, w_ref, b_ref, o_ref, xs_ref, *, segs, kw, rc):
    half = kw // 2
    cw = x_ref.shape[-1]
    zeros = jnp.zeros((_CONV_PAD, cw), F32)
    off = 0
    bases = []
    for st, ln in segs:
        xs_ref[off:off + _CONV_PAD, :] = zeros
        xs_ref[off + _CONV_PAD:off + _CONV_PAD + ln, :] = x_ref[st:st + ln, :].astype(F32)
        bases.append(off + _CONV_PAD)
        off += _CONV_PAD + ln
    xs_ref[off:off + _CONV_PAD, :] = zeros
    for (st, ln), base in zip(segs, bases):
        for r0 in range(0, ln, rc):
            acc = b_ref[...] + w_ref[0:1, :] * xs_ref[base + r0 - half:base + r0 - half + rc, :]
            for k in range(1, kw):
                lo = base + r0 + k - half
                acc = acc + w_ref[k:k + 1, :] * xs_ref[lo:lo + rc, :]
            o_ref[st + r0:st + r0 + rc, :] = (acc * _sigmoid(acc)).astype(o_ref.dtype)


def _conv_silu(p_main, conv_w, conv_b, col_off, segs):
    b, s, _ = p_main.shape
    kw, xbc = conv_w.shape
    cw = 512 if (xbc % 512 == 0 and col_off % 512 == 0) else LANES
    assert xbc % cw == 0 and col_off % cw == 0 and kw // 2 <= _CONV_PAD
    rc = _tile(min(ln for _, ln in segs), 256)
    assert all(ln % rc == 0 for _, ln in segs)
    rows = s + _CONV_PAD * (len(segs) + 1)
    ob = col_off // cw
    return pl.pallas_call(
        functools.partial(_conv_kernel, segs=segs, kw=kw, rc=rc),
        out_shape=jax.ShapeDtypeStruct((b, s, xbc), BF16),
        grid=(b, xbc // cw),
        in_specs=[
            pl.BlockSpec((None, s, cw), lambda bi, j: (bi, 0, ob + j)),
            pl.BlockSpec((kw, cw), lambda bi, j: (0, j)),
            pl.BlockSpec((1, cw), lambda bi, j: (0, j)),
        ],
        out_specs=pl.BlockSpec((None, s, cw), lambda bi, j: (bi, 0, j)),
        scratch_shapes=[pltpu.VMEM((rows, cw), F32)],
        compiler_params=_cparams(("parallel", "parallel")),
        name="conv_silu",
    )(p_main, conv_w, conv_b.reshape(1, xbc))


def _head_rows(v_t, c0, hpg, p):
    q = v_t.shape[1]
    return jnp.concatenate(
        [jnp.broadcast_to(v_t[c0 + h:c0 + h + 1, :], (p, q)) for h in range(hpg)], axis=0)


def _ssd_dir(xa_ref, dtr_ref, arow, dtb_row, y_ref, st_ref, *, backward, nh, ng, p, n, q):
    hpg = nh // ng
    width = nh * p
    gw = hpg * p
    col0 = nh if backward else 0
    dt = _softplus(dtr_ref[...] + dtb_row)
    da = dt * arow
    ri = lax.broadcasted_iota(I32, (q, q), 0)
    ci = lax.broadcasted_iota(I32, (q, q), 1)
    tri = jnp.where((ri <= ci) if backward else (ri >= ci), 1.0, 0.0).astype(F32)
    cs = jnp.dot(tri, da, preferred_element_type=F32, precision=lax.Precision.HIGHEST)
    tot = cs[0:1, :] if backward else cs[q - 1:q, :]
    cdec = jnp.exp(tot)
    dt_t = dt.T
    cs_t = cs.T
    wts_t = (dt * jnp.exp(tot - cs)).T
    ecs_t = jnp.exp(cs).T
    mask_t = (ci <= ri) if backward else (ci >= ri)
    for g in range(ng):
        x_t = xa_ref[:, g * gw:(g + 1) * gw].astype(F32).T
        bg = xa_ref[:, width + g * n:width + (g + 1) * n]
        cg = xa_ref[:, width + ng * n + g * n:width + ng * n + (g + 1) * n]
        c_t = cg.astype(F32).T.astype(BF16)
        cb_t = jnp.dot(bg, c_t, preferred_element_type=F32)
        c0 = col0 + g * hpg
        yd = []
        for h in range(hpg):
            c = c0 + h
            seg = cs_t[c:c + 1, :] - cs[:, c:c + 1]
            m_t = (cb_t * jnp.exp(jnp.where(mask_t, seg, MASK_NEG))).astype(BF16)
            xdt_t = (x_t[h * p:(h + 1) * p, :] * dt_t[c:c + 1, :]).astype(BF16)
            yd.append(jnp.dot(xdt_t, m_t, preferred_element_type=F32))
        st_old = st_ref[g]
        yo_t = jnp.dot(st_old.astype(BF16), c_t, preferred_element_type=F32)
        y_t = jnp.concatenate(yd, axis=0) + yo_t * _head_rows(ecs_t, c0, hpg, p)
        xw_t = (x_t * _head_rows(wts_t, c0, hpg, p)).astype(BF16)
        cdec_rows = jnp.concatenate(
            [jnp.broadcast_to(cdec[:, c0 + h:c0 + h + 1], (p, n)) for h in range(hpg)], axis=0)
        st_ref[g] = st_old * cdec_rows + jnp.dot(xw_t, bg, preferred_element_type=F32)
        y_ref[:, g * gw:(g + 1) * gw] = y_t.T.astype(y_ref.dtype)


def _ssd_kernel(xf_ref, xb_ref, dtf_ref, dtb_ref, alog_ref, dtbias_ref, yf_ref, yb_ref,
                sf_ref, sb_ref, **kw):
    @pl.when(pl.program_id(1) == 0)
    def _():
        sf_ref[...] = jnp.zeros_like(sf_ref)
        sb_ref[...] = jnp.zeros_like(sb_ref)

    arow = -jnp.exp(alog_ref[...])
    dtb_row = dtbias_ref[...]
    _ssd_dir(xf_ref, dtf_ref, arow, dtb_row, yf_ref, sf_ref, backward=False, **kw)
    _ssd_dir(xb_ref, dtb_ref, arow, dtb_row, yb_ref, sb_ref, backward=True, **kw)


def _ssd(xact, dtraw, alog_row, dtbias_row, nh, p, n, ctx_len):
    b, s, xbc = xact.shape
    q = SSM_CHUNK
    ng = SSM_GROUPS
    width = nh * p
    ncc = ctx_len // q
    nc = s // q
    ncl = nc - ncc

    def fwd(bi, i):
        return (bi, i, 0)

    def bwd(bi, i):
        return (bi, jnp.where(i < ncc, ncc - 1 - i, 2 * ncc + ncl - 1 - i), 0)

    kern = functools.partial(_ssd_kernel, nh=nh, ng=ng, p=p, n=n, q=q)
    return pl.pallas_call(
        kern,
        out_shape=(jax.ShapeDtypeStruct((b, s, width), BF16), jax.ShapeDtypeStruct((b, s, width), BF16)),
        grid=(b, nc),
        in_specs=[
            pl.BlockSpec((None, q, xbc), fwd),
            pl.BlockSpec((None, q, xbc), bwd),
            pl.BlockSpec((None, q, LANES), fwd),
            pl.BlockSpec((None, q, LANES), bwd),
            pl.BlockSpec((1, LANES), lambda bi, i: (0, 0)),
            pl.BlockSpec((1, LANES), lambda bi, i: (0, 0)),
        ],
        out_specs=(pl.BlockSpec((None, q, width), fwd), pl.BlockSpec((None, q, width), bwd)),
        scratch_shapes=[pltpu.VMEM((ng, (nh // ng) * p, n), F32), pltpu.VMEM((ng, (nh // ng) * p, n), F32)],
        compiler_params=_cparams(("parallel", "arbitrary")),
        name="ssd_scan",
    )(xact, xact, dtraw, dtraw, alog_row, dtbias_row)


def _gnorm_kernel(yf_ref, yb_ref, x_ref, z_ref, d_ref, w_ref, o_ref, *, ng):
    x = x_ref[...].astype(F32)
    z = z_ref[...].astype(F32)
    y = yf_ref[...].astype(F32) + yb_ref[...].astype(F32) + d_ref[...] * x
    g = y * (z * _sigmoid(z))
    gw = g.shape[-1] // ng
    for k in range(ng):
        gk = g[:, k * gw:(k + 1) * gw]
        ms = jnp.mean(gk * gk, axis=-1, keepdims=True)
        o_ref[:, k * gw:(k + 1) * gw] = (
            gk * lax.rsqrt(ms + LN_EPS) * w_ref[:, k * gw:(k + 1) * gw]).astype(o_ref.dtype)


def _gated_rmsnorm(yf, yb, xact, p_main, d_exp, norm_w):
    b, s, width = yf.shape
    tr = _tile(s, 256)
    blk = pl.BlockSpec((None, tr, width), lambda bi, i: (bi, i, 0))
    row = pl.BlockSpec((1, width), lambda bi, i: (0, 0))
    return pl.pallas_call(
        functools.partial(_gnorm_kernel, ng=SSM_GROUPS),
        out_shape=jax.ShapeDtypeStruct((b, s, width), BF16),
        grid=(b, s // tr),
        in_specs=[blk, blk, blk, blk, row, row],
        out_specs=blk,
        compiler_params=_cparams(("parallel", "parallel")),
        name="gated_rmsnorm",
    )(yf, yb, xact, p_main, d_exp, norm_w)


def _gmlp_kernel(u_ref, v_ref, lng_ref, lnb_ref, ws_ref, bst_ref, o_ref, *, ng):
    v = _gelu_tanh(v_ref[...].astype(F32))
    vb = (_ln(v) * lng_ref[...] + lnb_ref[...]).astype(BF16)
    u = _gelu_tanh(u_ref[...].astype(F32))
    gw = u.shape[-1] // ng
    for g in range(ng):
        mixed = jnp.dot(ws_ref[g], vb[:, g * gw:(g + 1) * gw], preferred_element_type=F32)
        mixed = mixed + bst_ref[:, g:g + 1]
        o_ref[:, g * gw:(g + 1) * gw] = (u[:, g * gw:(g + 1) * gw] * mixed).astype(o_ref.dtype)


def _gmlp(p_main, off_u, off_v, ln_g, ln_b, w_s, b_s):
    b, s, _ = p_main.shape
    ng, q, _ = w_s.shape
    wg = ln_g.shape[-1]
    assert off_u % wg == 0 and off_v % wg == 0
    bu, bv = off_u // wg, off_v // wg
    row = pl.BlockSpec((1, wg), lambda bi, c: (0, 0))
    return pl.pallas_call(
        functools.partial(_gmlp_kernel, ng=ng),
        out_shape=jax.ShapeDtypeStruct((b, s, wg), BF16),
        grid=(b, s // q),
        in_specs=[
            pl.BlockSpec((None, q, wg), lambda bi, c: (bi, c, bu)),
            pl.BlockSpec((None, q, wg), lambda bi, c: (bi, c, bv)),
            row, row,
            pl.BlockSpec((ng, q, q), lambda bi, c: (0, 0, 0)),
            pl.BlockSpec((q, ng), lambda bi, c: (0, 0)),
        ],
        out_specs=pl.BlockSpec((None, q, wg), lambda bi, c: (bi, c, 0)),
        compiler_params=_cparams(("parallel", "parallel")),
        name="gmlp",
    )(p_main, p_main, ln_g.reshape(1, wg), ln_b.reshape(1, wg), w_s.astype(BF16), b_s.T)


def _merge_kernel(a_ref, g_ref, ga_ref, gb_ref, wpa_ref, wpb_ref, o_ref):
    pa = jnp.dot(a_ref[...], wpa_ref[...], preferred_element_type=F32)
    pb = jnp.dot(g_ref[...], wpb_ref[...], preferred_element_type=F32)
    m = _sigmoid(ga_ref[...].astype(F32)) * pa + _sigmoid(gb_ref[...].astype(F32)) * pb
    o_ref[...] = m.astype(o_ref.dtype)


def _merge(ssd_out, g_out, p_main2d, off_ga, off_gb, w_pa, w_pb):
    m, wa = ssd_out.shape
    wb = g_out.shape[1]
    n = w_pa.shape[1]
    tm = _tile(m, 512)
    tn = _tile(n, 1024, LANES)
    assert off_ga % tn == 0 and off_gb % tn == 0
    ba, bb = off_ga // tn, off_gb // tn
    return pl.pallas_call(
        _merge_kernel,
        out_shape=jax.ShapeDtypeStruct((m, n), BF16),
        grid=(n // tn, m // tm),
        in_specs=[
            pl.BlockSpec((tm, wa), lambda j, i: (i, 0)),
            pl.BlockSpec((tm, wb), lambda j, i: (i, 0)),
            pl.BlockSpec((tm, tn), lambda j, i: (i, ba + j)),
            pl.BlockSpec((tm, tn), lambda j, i: (i, bb + j)),
            pl.BlockSpec((wa, tn), lambda j, i: (0, j)),
            pl.BlockSpec((wb, tn), lambda j, i: (0, j)),
        ],
        out_specs=pl.BlockSpec((tm, tn), lambda j, i: (i, j)),
        compiler_params=_cparams(("parallel", "parallel")),
        name="merge_branches",
    )(ssd_out, g_out, p_main2d, p_main2d, w_pa, w_pb)


def _top_k(logits, k):
    lane = lax.broadcasted_iota(I32, logits.shape, 1)
    vals, idxs = [], []
    cur = logits
    for _ in range(k):
        m = jnp.max(cur, axis=-1, keepdims=True)
        idx = jnp.min(jnp.where(cur == m, lane, LANES), axis=-1, keepdims=True)
        vals.append(m)
        idxs.append(idx)
        cur = jnp.where(lane == idx, -jnp.inf, cur)
    return vals, idxs


def _wo_res_kernel(m_ref, wo_ref, s_ref, mod_ref, lng_ref, lnb_ref, wr_ref, br_ref,
                   snew_ref, h_ref, idx_ref, wts_ref, *, alpha, k):
    y = jnp.dot(m_ref[...], wo_ref[...], preferred_element_type=F32)
    t = alpha * s_ref[...] + mod_ref[2:3, :] * y
    sn = _ln(t) * lng_ref[...] + lnb_ref[...]
    snew_ref[...] = sn
    h = _ln(sn) * (1.0 + mod_ref[4:5, :]) + mod_ref[3:4, :]
    h_ref[...] = _pack_bf16_pair(h)
    logits = jnp.dot(h.astype(BF16), wr_ref[...], preferred_element_type=F32) + br_ref[...]
    vals, idxs = _top_k(logits, k)
    es = [jnp.exp(v - vals[0]) for v in vals]
    den = es[0]
    for e in es[1:]:
        den = den + e
    for j in range(k):
        idx_ref[:, j:j + 1] = idxs[j]
        wts_ref[:, j:j + 1] = es[j] / den


def _wo_residual(m_all, w_o, s_all, mod, ln_g, ln_b, w_router, b_router, ctx_len, alpha):
    b, s, d = s_all.shape
    tr = _tile(ctx_len, 256)
    nct = ctx_len // tr
    blk = lambda w: pl.BlockSpec((None, tr, w), lambda bi, i: (bi, i, 0))
    row = pl.BlockSpec((1, d), lambda bi, i: (0, 0))
    return pl.pallas_call(
        functools.partial(_wo_res_kernel, alpha=alpha, k=TOP_K),
        out_shape=(
            jax.ShapeDtypeStruct((b, s, d), F32),
            jax.ShapeDtypeStruct((b, s, d // 2), U32),
            jax.ShapeDtypeStruct((b, s, TOP_K), I32),
            jax.ShapeDtypeStruct((b, s, TOP_K), F32),
        ),
        grid=(b, s // tr),
        in_specs=[
            blk(d),
            pl.BlockSpec((d, d), lambda bi, i: (0, 0)),
            blk(d),
            pl.BlockSpec((None, None, N_MOD, d), lambda bi, i: (bi, jnp.minimum(i // nct, 1), 0, 0)),
            row, row,
            pl.BlockSpec((d, LANES), lambda bi, i: (0, 0)),
            pl.BlockSpec((1, LANES), lambda bi, i: (0, 0)),
        ],
        out_specs=(blk(d), blk(d // 2), blk(TOP_K), blk(TOP_K)),
        compiler_params=_cparams(("parallel", "parallel")),
        name="wo_residual_router",
    )(m_all, w_o, s_all, mod, ln_g, ln_b, w_router, b_router)


def _moe_plan(idx, ne, tm):
    m, k = idx.shape
    na = m * k
    assert na % tm == 0
    e = idx.reshape(na)
    onehot = (e[:, None] == jnp.arange(ne, dtype=I32)[None, :]).astype(I32)
    csum = jnp.cumsum(onehot, axis=0)
    rank = jnp.sum(csum * onehot, axis=1) - 1
    counts = csum[-1]
    ptiles = (counts + tm - 1) // tm
    tile_end = jnp.cumsum(ptiles)
    tile_start = tile_end - ptiles
    pos = (jnp.sum(onehot * tile_start[None, :], axis=1) * tm + rank).astype(I32)
    nt = na // tm + ne
    tids = jnp.arange(nt, dtype=I32)
    te = jnp.minimum(jnp.sum((tids[:, None] >= tile_end[None, :]).astype(I32), axis=1), ne - 1)
    return pos, te.astype(I32), tile_end[-1:].astype(I32)


_ISSUE_UNROLL = 8


def _dispatch_kernel(pos_ref, h_ref, xs_in, xs_out, sem, *, k):
    del xs_in
    tr = h_ref.shape[0]

    def issue(i, c):
        for u in range(_ISSUE_UNROLL):
            r = i * _ISSUE_UNROLL + u
            for j in range(k):
                pltpu.make_async_copy(
                    h_ref.at[pl.ds(r, 1)], xs_out.at[pl.ds(pos_ref[0, r * k + j], 1)], sem
                ).start(priority=j % 2)
        return c

    lax.fori_loop(0, tr // _ISSUE_UNROLL, issue, 0)
    for j in range(k):
        pltpu.make_async_copy(h_ref, xs_out.at[pl.ds(0, tr)], sem).wait()


def _dispatch(h_packed, pos, xs):
    m, dh = h_packed.shape
    k = pos.shape[0] // m
    tr = _tile(m, 256, _ISSUE_UNROLL)
    return pl.pallas_call(
        functools.partial(_dispatch_kernel, k=k),
        out_shape=jax.ShapeDtypeStruct(xs.shape, xs.dtype),
        grid=(m // tr,),
        in_specs=[
            pl.BlockSpec((None, 1, tr * k), lambda i: (i, 0, 0), memory_space=pltpu.SMEM),
            pl.BlockSpec((tr, dh), lambda i: (i, 0)),
            pl.BlockSpec(memory_space=pl.ANY),
        ],
        out_specs=pl.BlockSpec(memory_space=pl.ANY),
        scratch_shapes=[pltpu.SemaphoreType.DMA(())],
        input_output_aliases={2: 0},
        compiler_params=_cparams(("arbitrary",)),
        name="moe_dispatch",
    )(pos.reshape(m // tr, 1, tr * k), h_packed, xs)


def _expert_kernel(te_ref, nact_ref, x_ref, wup_ref, bup_ref, wdn_ref, bdn_ref, y_ref):
    active = pl.program_id(0) < nact_ref[0]

    @pl.when(jnp.logical_not(active))
    def _():
        y_ref[...] = jnp.zeros_like(y_ref)

    @pl.when(active)
    def _():
        dh = x_ref.shape[-1]
        lo, hi = _unpack_bf16_pair(x_ref[...])
        hid = jnp.dot(lo.astype(BF16), wup_ref[0:dh, :], preferred_element_type=F32)
        hid = hid + jnp.dot(hi.astype(BF16), wup_ref[dh:, :], preferred_element_type=F32)
        hid = hid + bup_ref[...]
        f = hid.shape[-1] // 2
        glu = jnp.minimum(hid[:, :f], SWIGLU_LIMIT)
        lin = jnp.clip(hid[:, f:], -SWIGLU_LIMIT, SWIGLU_LIMIT)
        act = (glu * _sigmoid(SWIGLU_ALPHA * glu) * (lin + 1.0)).astype(BF16)
        y = jnp.dot(act, wdn_ref[...], preferred_element_type=F32) + bdn_ref[...]
        y_ref[...] = _pack_bf16_pair(y)


def _experts(xs, te, nact, w_up, b_up, w_down, b_down, tm):
    p_rows, dh = xs.shape
    ne, d, f2 = w_up.shape
    f = f2 // 2
    nt = p_rows // tm

    def row_map(t, te, nact):
        return (jnp.minimum(t, nact[0] - 1), 0)

    def w_map(t, te, nact):
        return (te[t], 0, 0)

    grid_spec = pltpu.PrefetchScalarGridSpec(
        num_scalar_prefetch=2,
        grid=(nt,),
        in_specs=[
            pl.BlockSpec((tm, dh), row_map),
            pl.BlockSpec((None, d, f2), w_map),
            pl.BlockSpec((None, 1, f2), w_map),
            pl.BlockSpec((None, f, d), w_map),
            pl.BlockSpec((None, 1, d), w_map),
        ],
        out_specs=pl.BlockSpec((tm, dh), lambda t, te, nact: (t, 0)),
    )
    return pl.pallas_call(
        _expert_kernel,
        out_shape=jax.ShapeDtypeStruct((p_rows, dh), U32),
        grid_spec=grid_spec,
        compiler_params=_cparams(("arbitrary",)),
        name="moe_experts",
    )(te, nact, xs, w_up, b_up.reshape(ne, 1, f2), w_down, b_down.reshape(ne, 1, d))


def _combine_kernel(pos_ref, posn_ref, wts_ref, s_ref, mod_ref, lng_ref, lnb_ref, ys_hbm,
                    snew_ref, ybuf, sems, *, alpha, k):
    i = pl.program_id(0)
    n = pl.num_programs(0)
    tr = s_ref.shape[0]
    slot = lax.rem(i, 2)

    def issue(p_ref, sl):
        def body(it, c):
            for u in range(_ISSUE_UNROLL):
                r = it * _ISSUE_UNROLL + u
                for j in range(k):
                    pltpu.make_async_copy(
                        ys_hbm.at[pl.ds(p_ref[0, r * k + j], 1)], ybuf.at[sl, j, pl.ds(r, 1)], sems.at[sl]
                    ).start(priority=j % 2)
            return c

        lax.fori_loop(0, tr // _ISSUE_UNROLL, body, 0)

    @pl.when(i == 0)
    def _():
        issue(pos_ref, 0)

    @pl.when(i + 1 < n)
    def _():
        issue(posn_ref, 1 - slot)

    for j in range(k):
        pltpu.make_async_copy(ys_hbm.at[pl.ds(0, tr)], ybuf.at[slot, j], sems.at[slot]).wait()

    acc = None
    for j in range(k):
        lo, hi = _unpack_bf16_pair(ybuf[slot, j])
        yk = wts_ref[:, j:j + 1] * jnp.concatenate([lo, hi], axis=1)
        acc = yk if acc is None else acc + yk
    t = alpha * s_ref[...] + mod_ref[5:6, :] * acc
    snew_ref[...] = _ln(t) * lng_ref[...] + lnb_ref[...]


def _combine(ys, pos, wts, s2, mod, ln_g, ln_b, seg_rows, ctx_len, alpha):
    m, d = s2.shape
    k = wts.shape[1]
    dh = ys.shape[1]
    tr = _tile(ctx_len, 256, _ISSUE_UNROLL)
    nt = m // tr
    spt = seg_rows // tr
    nct = ctx_len // tr
    pos3 = pos.reshape(nt, 1, tr * k)
    row = pl.BlockSpec((1, d), lambda i: (0, 0))
    return pl.pallas_call(
        functools.partial(_combine_kernel, alpha=alpha, k=k),
        out_shape=jax.ShapeDtypeStruct((m, d), F32),
        grid=(nt,),
        in_specs=[
            pl.BlockSpec((None, 1, tr * k), lambda i: (i, 0, 0), memory_space=pltpu.SMEM),
            pl.BlockSpec((None, 1, tr * k), lambda i: (jnp.minimum(i + 1, nt - 1), 0, 0),
                         memory_space=pltpu.SMEM),
            pl.BlockSpec((tr, k), lambda i: (i, 0)),
            pl.BlockSpec((tr, d), lambda i: (i, 0)),
            pl.BlockSpec((None, None, N_MOD, d),
                         lambda i: (i // spt, jnp.minimum((i % spt) // nct, 1), 0, 0)),
            row, row,
            pl.BlockSpec(memory_space=pl.ANY),
        ],
        out_specs=pl.BlockSpec((tr, d), lambda i: (i, 0)),
        scratch_shapes=[pltpu.VMEM((2, k, tr, dh), U32), pltpu.SemaphoreType.DMA((2,))],
        compiler_params=_cparams(("arbitrary",)),
        name="moe_combine_residual",
    )(pos3, pos3, wts, s2, mod, ln_g, ln_b, ys)


def _to_column_major(t, ctx_len):
    b, s, d = t.shape
    lat = t[:, ctx_len:]
    rows = (s - ctx_len) // GRID_W
    lat = lat.reshape(b, rows, GRID_W, d).swapaxes(1, 2).reshape(b, s - ctx_len, d)
    return jnp.concatenate([t[:, :ctx_len], lat], axis=1)


def _to_row_major(t, ctx_len):
    b, s, d = t.shape
    lat = t[:, ctx_len:]
    rows = (s - ctx_len) // GRID_W
    lat = lat.reshape(b, GRID_W, rows, d).swapaxes(1, 2).reshape(b, s - ctx_len, d)
    return jnp.concatenate([t[:, :ctx_len], lat], axis=1)


def kernel(x, c, ctx, c_ctx, w_ada, b_ada, w_in, b_in, conv_w, conv_b, a_log_f, a_log_b, dt_bias_f, dt_bias_b, d_skip, ssm_norm_w, gmlp_ln_g, gmlp_ln_b, w_s, b_s, w_pa, w_pb, w_o, ln_g, ln_b, w_router, b_router, w_up, b_up, w_down, b_down):
    bsz, seq, d = x.shape
    ctx_len = ctx.shape[1]
    depth = w_ada.shape[0]
    s_tot = ctx_len + seq
    m_tot = bsz * s_tot
    nh = a_log_f.shape[1]
    ssm_w = ssm_norm_w.shape[1]
    p = ssm_w // nh
    xbc = conv_w.shape[2]
    n_state = (xbc - ssm_w) // (2 * SSM_GROUPS)
    gw = gmlp_ln_g.shape[1]
    ne = w_router.shape[2]
    assert 2 * nh <= LANES and ne <= LANES

    alpha = (2 * depth) ** 0.25
    off_xbc = ssm_w
    off_dtf = off_xbc + xbc
    off_u = off_dtf + 2 * nh
    off_v = off_u + gw
    off_ga = off_v + gw
    off_gb = off_ga + d
    m_off_u = off_dtf
    m_off_v = m_off_u + gw
    m_off_ga = m_off_v + gw
    m_off_gb = m_off_ga + d

    segs = ((0, ctx_len), (ctx_len, seq))
    tm_moe = 512 if (m_tot * TOP_K) % 512 == 0 else 128

    c_all = jnp.concatenate([c, c_ctx[None, :]], axis=0)
    mods = _ada(c_all, w_ada, b_ada).reshape(depth, bsz + 1, N_MOD, d)

    s_all = jnp.concatenate([ctx, x], axis=1)
    xs = jnp.zeros((m_tot * TOP_K + ne * tm_moe, d // 2), U32)

    for l in range(depth):
        column_major = l % 2 == 1
        mod = jnp.stack(
            [jnp.broadcast_to(mods[l, bsz][None], (bsz, N_MOD, d)), mods[l, :bsz]], axis=1)

        w_main = jnp.concatenate([w_in[l][:, :off_dtf], w_in[l][:, off_u:]], axis=1).astype(BF16)
        b_main = jnp.concatenate([b_in[l][:off_dtf], b_in[l][off_u:]])[None, :]
        w_dt = jnp.pad(w_in[l][:, off_dtf:off_u], ((0, 0), (0, LANES - 2 * nh))).astype(BF16)
        b_dt = jnp.pad(b_in[l][off_dtf:off_u], (0, LANES - 2 * nh))[None, :]

        h = _ln_mod(s_all, mod, ctx_len, 0, 1)
        if column_major:
            h = _to_column_major(h, ctx_len)
        h2 = h.reshape(m_tot, d)
        p_main = _matmul(h2, w_main, b_main, BF16, "in_proj")
        dtraw = _matmul(h2, w_dt, b_dt, F32, "dt_proj").reshape(bsz, s_tot, LANES)
        p3 = p_main.reshape(bsz, s_tot, -1)

        xact = _conv_silu(p3, conv_w[l], conv_b[l], off_xbc, segs)
        alog_row = jnp.pad(jnp.concatenate([a_log_f[l], a_log_b[l]]), (0, LANES - 2 * nh))[None, :]
        dtb_row = jnp.pad(jnp.concatenate([dt_bias_f[l], dt_bias_b[l]]), (0, LANES - 2 * nh))[None, :]
        yf, yb = _ssd(xact, dtraw, alog_row, dtb_row, nh, p, n_state, ctx_len)
        d_exp = jnp.repeat(d_skip[l], p)[None, :]
        ssd_out = _gated_rmsnorm(yf, yb, xact, p3, d_exp, ssm_norm_w[l][None, :])
        g_out = _gmlp(p3, m_off_u, m_off_v, gmlp_ln_g[l], gmlp_ln_b[l], w_s[l], b_s[l])
        merged = _merge(ssd_out.reshape(m_tot, ssm_w), g_out.reshape(m_tot, gw), p_main,
                        m_off_ga, m_off_gb, w_pa[l].astype(BF16), w_pb[l].astype(BF16))
        merged = merged.reshape(bsz, s_tot, d)
        if column_major:
            merged = _to_row_major(merged, ctx_len)

        w_r = jnp.pad(w_router[l], ((0, 0), (0, LANES - ne))).astype(BF16)
        b_r = jnp.pad(b_router[l], (0, LANES - ne), constant_values=MASK_NEG)[None, :]
        s_all, h_moe, idx, wts = _wo_residual(
            merged, w_o[l].astype(BF16), s_all, mod, ln_g[l, 0][None, :], ln_b[l, 0][None, :],
            w_r, b_r, ctx_len, alpha)

        pos, te, nact = _moe_plan(idx.reshape(m_tot, TOP_K), ne, tm_moe)
        xs = _dispatch(h_moe.reshape(m_tot, d // 2), pos, xs)
        ys = _experts(xs, te, nact, w_up[l].astype(BF16), b_up[l], w_down[l].astype(BF16), b_down[l], tm_moe)
        s_all = _combine(ys, pos, wts.reshape(m_tot, TOP_K), s_all.reshape(m_tot, d), mod,
                         ln_g[l, 1][None, :], ln_b[l, 1][None, :], s_tot, ctx_len, alpha)
        s_all = s_all.reshape(bsz, s_tot, d)

    return s_all[:, ctx_len:]
```

```python
import functools

import jax
import jax.numpy as jnp
from jax import lax
from jax.experimental import pallas as pl
from jax.experimental.pallas import tpu as pltpu

F32 = jnp.float32
BF16 = jnp.bfloat16
U32 = jnp.uint32
I32 = jnp.int32

SSM_GROUPS = 8
SSM_CHUNK = 128
TOP_K = 4
GRID_W = 64
N_MOD = 6
SWIGLU_LIMIT = 7.0
SWIGLU_ALPHA = 1.702
LN_EPS = 1e-5

LANES = 128
VMEM_LIMIT = 56 * 1024 * 1024
MASK_NEG = -1e30


def _cparams(sem):
    return pltpu.CompilerParams(dimension_semantics=sem, vmem_limit_bytes=VMEM_LIMIT)


def _tile(n, pref, mult=8):
    if n <= pref:
        return n
    t = (pref // mult) * mult
    while t >= mult:
        if n % t == 0:
            return t
        t -= mult
    raise ValueError(f"no tile for {n} <= {pref}")


def _ln(x):
    mu = jnp.mean(x, axis=-1, keepdims=True)
    xc = x - mu
    var = jnp.mean(xc * xc, axis=-1, keepdims=True)
    return xc * lax.rsqrt(var + LN_EPS)


def _sigmoid(x):
    return 1.0 / (1.0 + jnp.exp(-x))


def _gelu_tanh(x):
    c = 0.7978845608028654
    return 0.5 * x * (1.0 + jnp.tanh(c * (x + 0.044715 * (x * x * x))))


def _softplus(x):
    return jnp.maximum(x, 0.0) + jnp.log(1.0 + jnp.exp(-jnp.abs(x)))


def _pack_bf16_pair(y):
    dh = y.shape[-1] // 2
    lo = lax.bitcast_convert_type(y[:, :dh].astype(BF16).astype(F32), U32)
    hi = lax.bitcast_convert_type(y[:, dh:].astype(BF16).astype(F32), U32)
    return (hi & jnp.uint32(0xFFFF0000)) | (lo >> 16)


def _unpack_bf16_pair(w):
    lo = lax.bitcast_convert_type(w << 16, F32)
    hi = lax.bitcast_convert_type(w & jnp.uint32(0xFFFF0000), F32)
    return lo, hi


def _ada_kernel(c_ref, w_ref, b_ref, o_ref):
    c = c_ref[...]
    a = (c * _sigmoid(c)).astype(BF16)
    o_ref[...] = jnp.dot(a, w_ref[...].astype(BF16), preferred_element_type=F32) + b_ref[...]


def _ada(c_all, w_ada, b_ada):
    depth, d, n = w_ada.shape
    r = c_all.shape[0]
    tn = _tile(n, 1024, LANES)
    return pl.pallas_call(
        _ada_kernel,
        out_shape=jax.ShapeDtypeStruct((depth, r, n), F32),
        grid=(depth, n // tn),
        in_specs=[
            pl.BlockSpec((r, d), lambda l, j: (0, 0)),
            pl.BlockSpec((None, d, tn), lambda l, j: (l, 0, j)),
            pl.BlockSpec((None, 1, tn), lambda l, j: (l, 0, j)),
        ],
        out_specs=pl.BlockSpec((None, r, tn), lambda l, j: (l, 0, j)),
        compiler_params=_cparams(("parallel", "parallel")),
        name="ada_mod",
    )(c_all, w_ada, b_ada.reshape(depth, 1, n))


def _ln_mod_kernel(s_ref, mod_ref, h_ref, *, shift_row, scale_row):
    y = _ln(s_ref[...])
    scale = mod_ref[scale_row:scale_row + 1, :]
    shift = mod_ref[shift_row:shift_row + 1, :]
    h_ref[...] = (y * (1.0 + scale) + shift).astype(h_ref.dtype)


def _ln_mod(s_all, mod, ctx_len, shift_row, scale_row):
    b, s, d = s_all.shape
    tr = _tile(ctx_len, 256)
    nct = ctx_len // tr
    return pl.pallas_call(
        functools.partial(_ln_mod_kernel, shift_row=shift_row, scale_row=scale_row),
        out_shape=jax.ShapeDtypeStruct((b, s, d), BF16),
        grid=(b, s // tr),
        in_specs=[
            pl.BlockSpec((None, tr, d), lambda bi, i: (bi, i, 0)),
            pl.BlockSpec((None, None, N_MOD, d), lambda bi, i: (bi, jnp.minimum(i // nct, 1), 0, 0)),
        ],
        out_specs=pl.BlockSpec((None, tr, d), lambda bi, i: (bi, i, 0)),
        compiler_params=_cparams(("parallel", "parallel")),
        name="ln_mod",
    )(s_all, mod)


def _mm_kernel(a_ref, w_ref, b_ref, o_ref):
    acc = jnp.dot(a_ref[...], w_ref[...], preferred_element_type=F32)
    o_ref[...] = (acc + b_ref[...]).astype(o_ref.dtype)


def _matmul(a, w, bias, out_dtype, name):
    m, k = a.shape
    n = w.shape[1]
    tm = _tile(m, 1024)
    tn = _tile(n, 1024, LANES)
    return pl.pallas_call(
        _mm_kernel,
        out_shape=jax.ShapeDtypeStruct((m, n), out_dtype),
        grid=(n // tn, m // tm),
        in_specs=[
            pl.BlockSpec((tm, k), lambda j, i: (i, 0)),
            pl.BlockSpec((k, tn), lambda j, i: (0, j)),
            pl.BlockSpec((1, tn), lambda j, i: (0, j)),
        ],
        out_specs=pl.BlockSpec((tm, tn), lambda j, i: (i, j)),
        compiler_params=_cparams(("parallel", "parallel")),
        name=name,
    )(a, w, bias)


_CONV_ROWS = 128
_CONV_WIN = 256


def _conv_kernel(x_ref, w_ref, b_ref, o_ref, *, segs, kw):
    half = kw // 2
    rb = _CONV_ROWS
    shift_mats = {}

    taps = [k for k in range(kw) if k != half]

    def shifts(win, off):
        if (win, off) not in shift_mats:
            t = lax.broadcasted_iota(I32, (rb, win), 0)
            j = lax.broadcasted_iota(I32, (rb, win), 1)
            shift_mats[(win, off)] = jnp.concatenate(
                [jnp.where(j == t + (off + k - half), 1.0, 0.0).astype(BF16) for k in taps], axis=0)
        return shift_mats[(win, off)]

    for st, ln in segs:
        win = min(_CONV_WIN, ln)
        for r0 in range(st, st + ln, rb):
            lo = min(max(r0 - (win - rb) // 2, st), st + ln - win)
            shifted = jnp.dot(shifts(win, r0 - lo), x_ref[lo:lo + win, :], preferred_element_type=F32)
            acc = b_ref[...] + w_ref[half:half + 1, :] * x_ref[r0:r0 + rb, :].astype(F32)
            for n, k in enumerate(taps):
                acc = acc + w_ref[k:k + 1, :] * shifted[n * rb:(n + 1) * rb, :]
            o_ref[r0:r0 + rb, :] = (acc * _sigmoid(acc)).astype(o_ref.dtype)


def _conv_silu(p_main, conv_w, conv_b, col_off, segs):
    b, s, _ = p_main.shape
    kw, xbc = conv_w.shape
    cw = 512 if (xbc % 512 == 0 and col_off % 512 == 0) else LANES
    assert xbc % cw == 0 and col_off % cw == 0
    assert all(ln % _CONV_ROWS == 0 and (ln >= _CONV_WIN or ln == _CONV_ROWS) for _, ln in segs)
    assert kw // 2 <= (_CONV_WIN - _CONV_ROWS) // 2
    ob = col_off // cw
    return pl.pallas_call(
        functools.partial(_conv_kernel, segs=segs, kw=kw),
        out_shape=jax.ShapeDtypeStruct((b, s, xbc), BF16),
        grid=(b, xbc // cw),
        in_specs=[
            pl.BlockSpec((None, s, cw), lambda bi, j: (bi, 0, ob + j)),
            pl.BlockSpec((kw, cw), lambda bi, j: (0, j)),
            pl.BlockSpec((1, cw), lambda bi, j: (0, j)),
        ],
        out_specs=pl.BlockSpec((None, s, cw), lambda bi, j: (bi, 0, j)),
        compiler_params=_cparams(("parallel", "parallel")),
        name="conv_silu",
    )(p_main, conv_w, conv_b.reshape(1, xbc))


def _head_rows(v_t, c0, hpg, p):
    q = v_t.shape[1]
    return jnp.concatenate(
        [jnp.broadcast_to(v_t[c0 + h:c0 + h + 1, :], (p, q)) for h in range(hpg)], axis=0)


def _ssd_dir(xa_ref, dtr_ref, arow, dtb_row, y_ref, st_ref, *, backward, nh, ng, p, n, q):
    hpg = nh // ng
    width = nh * p
    gw = hpg * p
    col0 = nh if backward else 0
    dt = _softplus(dtr_ref[...] + dtb_row)
    da = dt * arow
    ri = lax.broadcasted_iota(I32, (q, q), 0)
    ci = lax.broadcasted_iota(I32, (q, q), 1)
    tri = jnp.where((ri <= ci) if backward else (ri >= ci), 1.0, 0.0).astype(F32)
    cs = jnp.dot(tri, da, preferred_element_type=F32, precision=lax.Precision.HIGHEST)
    tot = cs[0:1, :] if backward else cs[q - 1:q, :]
    cdec = jnp.exp(tot)
    dt_t = dt.T
    cs_t = cs.T
    wts_t = (dt * jnp.exp(tot - cs)).T
    ecs_t = jnp.exp(cs).T
    mask_t = (ci <= ri) if backward else (ci >= ri)
    for g in range(ng):
        x_t = xa_ref[:, g * gw:(g + 1) * gw].astype(F32).T
        bg = xa_ref[:, width + g * n:width + (g + 1) * n]
        cg = xa_ref[:, width + ng * n + g * n:width + ng * n + (g + 1) * n]
        c_t = cg.astype(F32).T.astype(BF16)
        cb_t = jnp.dot(bg, c_t, preferred_element_type=F32)
        c0 = col0 + g * hpg
        yd = []
        for h in range(hpg):
            c = c0 + h
            seg = cs_t[c:c + 1, :] - cs[:, c:c + 1]
            m_t = (cb_t * jnp.exp(jnp.where(mask_t, seg, MASK_NEG))).astype(BF16)
            xdt_t = (x_t[h * p:(h + 1) * p, :] * dt_t[c:c + 1, :]).astype(BF16)
            yd.append(jnp.dot(xdt_t, m_t, preferred_element_type=F32))
        st_old = st_ref[g]
        yo_t = jnp.dot(st_old.astype(BF16), c_t, preferred_element_type=F32)
        y_t = jnp.concatenate(yd, axis=0) + yo_t * _head_rows(ecs_t, c0, hpg, p)
        xw_t = (x_t * _head_rows(wts_t, c0, hpg, p)).astype(BF16)
        cdec_rows = jnp.concatenate(
            [jnp.broadcast_to(cdec[:, c0 + h:c0 + h + 1], (p, n)) for h in range(hpg)], axis=0)
        st_ref[g] = st_old * cdec_rows + jnp.dot(xw_t, bg, preferred_element_type=F32)
        y_ref[:, g * gw:(g + 1) * gw] = y_t.T.astype(y_ref.dtype)


def _ssd_kernel(xf_ref, xb_ref, dtf_ref, dtb_ref, alog_ref, dtbias_ref, yf_ref, yb_ref,
                sf_ref, sb_ref, **kw):
    @pl.when(pl.program_id(1) == 0)
    def _():
        sf_ref[...] = jnp.zeros_like(sf_ref)
        sb_ref[...] = jnp.zeros_like(sb_ref)

    arow = -jnp.exp(alog_ref[...])
    dtb_row = dtbias_ref[...]
    _ssd_dir(xf_ref, dtf_ref, arow, dtb_row, yf_ref, sf_ref, backward=False, **kw)
    _ssd_dir(xb_ref, dtb_ref, arow, dtb_row, yb_ref, sb_ref, backward=True, **kw)


def _ssd(xact, dtraw, alog_row, dtbias_row, nh, p, n, ctx_len):
    b, s, xbc = xact.shape
    q = SSM_CHUNK
    ng = SSM_GROUPS
    width = nh * p
    ncc = ctx_len // q
    nc = s // q
    ncl = nc - ncc

    def fwd(bi, i):
        return (bi, i, 0)

    def bwd(bi, i):
        return (bi, jnp.where(i < ncc, ncc - 1 - i, 2 * ncc + ncl - 1 - i), 0)

    kern = functools.partial(_ssd_kernel, nh=nh, ng=ng, p=p, n=n, q=q)
    return pl.pallas_call(
        kern,
        out_shape=(jax.ShapeDtypeStruct((b, s, width), BF16), jax.ShapeDtypeStruct((b, s, width), BF16)),
        grid=(b, nc),
        in_specs=[
            pl.BlockSpec((None, q, xbc), fwd),
            pl.BlockSpec((None, q, xbc), bwd),
            pl.BlockSpec((None, q, LANES), fwd),
            pl.BlockSpec((None, q, LANES), bwd),
            pl.BlockSpec((1, LANES), lambda bi, i: (0, 0)),
            pl.BlockSpec((1, LANES), lambda bi, i: (0, 0)),
        ],
        out_specs=(pl.BlockSpec((None, q, width), fwd), pl.BlockSpec((None, q, width), bwd)),
        scratch_shapes=[pltpu.VMEM((ng, (nh // ng) * p, n), F32), pltpu.VMEM((ng, (nh // ng) * p, n), F32)],
        compiler_params=_cparams(("parallel", "arbitrary")),
        name="ssd_scan",
    )(xact, xact, dtraw, dtraw, alog_row, dtbias_row)


_GNORM_ROWS = 128


def _gated_rmsnorm_rows(yf_ref, yb_ref, x_ref, z_ref, d_ref, w_ref, o_ref, r0, nr, ng):
    rows = slice(r0, r0 + nr)
    x = x_ref[rows, :].astype(F32)
    z = z_ref[rows, :].astype(F32)
    y = yf_ref[rows, :].astype(F32) + yb_ref[rows, :].astype(F32) + d_ref[...] * x
    g = y * (z * _sigmoid(z))
    gw = g.shape[-1] // ng
    for k in range(ng):
        gk = g[:, k * gw:(k + 1) * gw]
        ms = jnp.mean(gk * gk, axis=-1, keepdims=True)
        o_ref[rows, k * gw:(k + 1) * gw] = (
            gk * lax.rsqrt(ms + LN_EPS) * w_ref[:, k * gw:(k + 1) * gw]).astype(o_ref.dtype)


def _gmlp_kernel(u_ref, v_ref, lng_ref, lnb_ref, ws_ref, bst_ref, o_ref, *, ng):
    v = _gelu_tanh(v_ref[...].astype(F32))
    vb = (_ln(v) * lng_ref[...] + lnb_ref[...]).astype(BF16)
    u = _gelu_tanh(u_ref[...].astype(F32))
    gw = u.shape[-1] // ng
    for g in range(ng):
        mixed = jnp.dot(ws_ref[g], vb[:, g * gw:(g + 1) * gw], preferred_element_type=F32)
        mixed = mixed + bst_ref[:, g:g + 1]
        o_ref[:, g * gw:(g + 1) * gw] = (u[:, g * gw:(g + 1) * gw] * mixed).astype(o_ref.dtype)


def _gmlp(p_main, off_u, off_v, ln_g, ln_b, w_s, b_s):
    b, s, _ = p_main.shape
    ng, q, _ = w_s.shape
    wg = ln_g.shape[-1]
    assert off_u % wg == 0 and off_v % wg == 0
    bu, bv = off_u // wg, off_v // wg
    row = pl.BlockSpec((1, wg), lambda bi, c: (0, 0))
    return pl.pallas_call(
        functools.partial(_gmlp_kernel, ng=ng),
        out_shape=jax.ShapeDtypeStruct((b, s, wg), BF16),
        grid=(b, s // q),
        in_specs=[
            pl.BlockSpec((None, q, wg), lambda bi, c: (bi, c, bu)),
            pl.BlockSpec((None, q, wg), lambda bi, c: (bi, c, bv)),
            row, row,
            pl.BlockSpec((ng, q, q), lambda bi, c: (0, 0, 0)),
            pl.BlockSpec((q, ng), lambda bi, c: (0, 0)),
        ],
        out_specs=pl.BlockSpec((None, q, wg), lambda bi, c: (bi, c, 0)),
        compiler_params=_cparams(("parallel", "parallel")),
        name="gmlp",
    )(p_main, p_main, ln_g.reshape(1, wg), ln_b.reshape(1, wg), w_s.astype(BF16), b_s.T)


def _merge_kernel(yf_ref, yb_ref, x_ref, z_ref, d_ref, nw_ref, g_ref, ga_ref, gb_ref, wpa_ref, wpb_ref,
                  o_ref, a_scr, *, ng):
    @pl.when(pl.program_id(1) == 0)
    def _():
        for r0 in range(0, a_scr.shape[0], _GNORM_ROWS):
            _gated_rmsnorm_rows(yf_ref, yb_ref, x_ref, z_ref, d_ref, nw_ref, a_scr, r0, _GNORM_ROWS, ng)

    pa = jnp.dot(a_scr[...], wpa_ref[...], preferred_element_type=F32)
    pb = jnp.dot(g_ref[...], wpb_ref[...], preferred_element_type=F32)
    m = _sigmoid(ga_ref[...].astype(F32)) * pa + _sigmoid(gb_ref[...].astype(F32)) * pb
    o_ref[...] = m.astype(o_ref.dtype)


def _merge(yf, yb, xact2d, p_main2d, d_exp, norm_w, g_out, off_ga, off_gb, w_pa, w_pb):
    m, wa = yf.shape
    wb = g_out.shape[1]
    n = w_pa.shape[1]
    tm = _tile(m, 512, _GNORM_ROWS)
    tn = _tile(n, 1024, LANES)
    assert off_ga % tn == 0 and off_gb % tn == 0
    ba, bb = off_ga // tn, off_gb // tn
    rows_a = pl.BlockSpec((tm, wa), lambda i, j: (i, 0))
    vec_a = pl.BlockSpec((1, wa), lambda i, j: (0, 0))
    return pl.pallas_call(
        functools.partial(_merge_kernel, ng=SSM_GROUPS),
        out_shape=jax.ShapeDtypeStruct((m, n), BF16),
        grid=(m // tm, n // tn),
        in_specs=[
            rows_a, rows_a, rows_a, rows_a, vec_a, vec_a,
            pl.BlockSpec((tm, wb), lambda i, j: (i, 0)),
            pl.BlockSpec((tm, tn), lambda i, j: (i, ba + j)),
            pl.BlockSpec((tm, tn), lambda i, j: (i, bb + j)),
            pl.BlockSpec((wa, tn), lambda i, j: (0, j)),
            pl.BlockSpec((wb, tn), lambda i, j: (0, j)),
        ],
        out_specs=pl.BlockSpec((tm, tn), lambda i, j: (i, j)),
        scratch_shapes=[pltpu.VMEM((tm, wa), BF16)],
        compiler_params=_cparams(("parallel", "arbitrary")),
        name="merge_branches",
    )(yf, yb, xact2d, p_main2d, d_exp, norm_w, g_out, p_main2d, p_main2d, w_pa, w_pb)


def _top_k(logits, k):
    lane = lax.broadcasted_iota(I32, logits.shape, 1)
    vals, idxs = [], []
    cur = logits
    for _ in range(k):
        m = jnp.max(cur, axis=-1, keepdims=True)
        idx = jnp.min(jnp.where(cur == m, lane, LANES), axis=-1, keepdims=True)
        vals.append(m)
        idxs.append(idx)
        cur = jnp.where(lane == idx, -jnp.inf, cur)
    return vals, idxs


def _wo_res_kernel(m_ref, wo_ref, s_ref, modc_ref, modl_ref, lng_ref, lnb_ref, wr_ref, br_ref,
                   snew_ref, h_ref, idx_ref, wts_ref, *, alpha, k, sub, nctx):
    y = jnp.dot(m_ref[...], wo_ref[...], preferred_element_type=F32)
    first = pl.program_id(1) == 0
    for r in range(m_ref.shape[0] // sub):
        rows = slice(r * sub, (r + 1) * sub)
        mod = jnp.where(first, modc_ref[...], modl_ref[...]) if r < nctx else modl_ref[...]
        t = alpha * s_ref[rows, :] + mod[2:3, :] * y[rows, :]
        sn = _ln(t) * lng_ref[...] + lnb_ref[...]
        snew_ref[rows, :] = sn
        h = _ln(sn) * (1.0 + mod[4:5, :]) + mod[3:4, :]
        h_ref[rows, :] = _pack_bf16_pair(h)
        logits = jnp.dot(h.astype(BF16), wr_ref[...], preferred_element_type=F32) + br_ref[...]
        vals, idxs = _top_k(logits, k)
        es = [jnp.exp(v - vals[0]) for v in vals]
        den = es[0]
        for e in es[1:]:
            den = den + e
        for j in range(k):
            idx_ref[rows, j:j + 1] = idxs[j]
            wts_ref[rows, j:j + 1] = es[j] / den


_WO_ROWS = 768


def _wo_residual(m_all, w_o, s_all, mod, ln_g, ln_b, w_router, b_router, ctx_len, alpha):
    b, s, d = s_all.shape
    sub = _tile(ctx_len, 256)
    nctx = ctx_len // sub
    tr = sub * max(t for t in range(1, s // sub + 1)
                   if (s // sub) % t == 0 and t >= nctx and t * sub <= max(_WO_ROWS, ctx_len))
    blk = lambda w: pl.BlockSpec((None, tr, w), lambda bi, i: (bi, i, 0))
    row = pl.BlockSpec((1, d), lambda bi, i: (0, 0))
    once = dict(pipeline_mode=pl.Buffered(1))
    return pl.pallas_call(
        functools.partial(_wo_res_kernel, alpha=alpha, k=TOP_K, sub=sub, nctx=nctx),
        out_shape=(
            jax.ShapeDtypeStruct((b, s, d), F32),
            jax.ShapeDtypeStruct((b, s, d // 2), U32),
            jax.ShapeDtypeStruct((b, s, TOP_K), I32),
            jax.ShapeDtypeStruct((b, s, TOP_K), F32),
        ),
        grid=(b, s // tr),
        in_specs=[
            blk(d),
            pl.BlockSpec((d, d), lambda bi, i: (0, 0), **once),
            blk(d),
            pl.BlockSpec((None, None, N_MOD, d), lambda bi, i: (bi, 0, 0, 0)),
            pl.BlockSpec((None, None, N_MOD, d), lambda bi, i: (bi, 1, 0, 0)),
            row, row,
            pl.BlockSpec((d, LANES), lambda bi, i: (0, 0), **once),
            pl.BlockSpec((1, LANES), lambda bi, i: (0, 0)),
        ],
        out_specs=(blk(d), blk(d // 2), blk(TOP_K), blk(TOP_K)),
        compiler_params=_cparams(("parallel", "arbitrary")),
        name="wo_residual_router",
    )(m_all, w_o, s_all, mod, mod, ln_g, ln_b, w_router, b_router)


def _moe_plan(idx, ne, tm):
    m, k = idx.shape
    na = m * k
    assert na % tm == 0
    e = idx.reshape(na)
    onehot = (e[:, None] == jnp.arange(ne, dtype=I32)[None, :]).astype(I32)
    csum = jnp.cumsum(onehot, axis=0)
    rank = jnp.sum(csum * onehot, axis=1) - 1
    counts = csum[-1]
    ptiles = (counts + tm - 1) // tm
    tile_end = jnp.cumsum(ptiles)
    tile_start = tile_end - ptiles
    pos = (jnp.sum(onehot * tile_start[None, :], axis=1) * tm + rank).astype(I32)
    nt = na // tm + ne
    tids = jnp.arange(nt, dtype=I32)
    te = jnp.minimum(jnp.sum((tids[:, None] >= tile_end[None, :]).astype(I32), axis=1), ne - 1)
    return pos, te.astype(I32), tile_end[-1:].astype(I32)


_ISSUE_UNROLL = 8


def _dispatch_kernel(pos_ref, h_ref, xs_in, xs_out, sem, *, k):
    del xs_in
    tr = h_ref.shape[0]

    def issue(i, c):
        for u in range(_ISSUE_UNROLL):
            r = i * _ISSUE_UNROLL + u
            for j in range(k):
                pltpu.make_async_copy(
                    h_ref.at[pl.ds(r, 1)], xs_out.at[pl.ds(pos_ref[0, r * k + j], 1)], sem
                ).start(priority=j % 2)
        return c

    lax.fori_loop(0, tr // _ISSUE_UNROLL, issue, 0)
    for j in range(k):
        pltpu.make_async_copy(h_ref, xs_out.at[pl.ds(0, tr)], sem).wait()


def _dispatch(h_packed, pos, xs):
    m, dh = h_packed.shape
    k = pos.shape[0] // m
    tr = _tile(m, 256, _ISSUE_UNROLL)
    return pl.pallas_call(
        functools.partial(_dispatch_kernel, k=k),
        out_shape=jax.ShapeDtypeStruct(xs.shape, xs.dtype),
        grid=(m // tr,),
        in_specs=[
            pl.BlockSpec((None, 1, tr * k), lambda i: (i, 0, 0), memory_space=pltpu.SMEM),
            pl.BlockSpec((tr, dh), lambda i: (i, 0)),
            pl.BlockSpec(memory_space=pl.ANY),
        ],
        out_specs=pl.BlockSpec(memory_space=pl.ANY),
        scratch_shapes=[pltpu.SemaphoreType.DMA(())],
        input_output_aliases={2: 0},
        compiler_params=_cparams(("arbitrary",)),
        name="moe_dispatch",
    )(pos.reshape(m // tr, 1, tr * k), h_packed, xs)


def _expert_kernel(te_ref, nact_ref, x_ref, wup_ref, bup_ref, wdn_ref, bdn_ref, y_ref):
    active = pl.program_id(0) < nact_ref[0]

    @pl.when(jnp.logical_not(active))
    def _():
        y_ref[...] = jnp.zeros_like(y_ref)

    @pl.when(active)
    def _():
        dh = x_ref.shape[-1]
        lo, hi = _unpack_bf16_pair(x_ref[...])
        hid = jnp.dot(lo.astype(BF16), wup_ref[0:dh, :], preferred_element_type=F32)
        hid = hid + jnp.dot(hi.astype(BF16), wup_ref[dh:, :], preferred_element_type=F32)
        hid = hid + bup_ref[...]
        f = hid.shape[-1] // 2
        glu = jnp.minimum(hid[:, :f], SWIGLU_LIMIT)
        lin = jnp.clip(hid[:, f:], -SWIGLU_LIMIT, SWIGLU_LIMIT)
        act = (glu * _sigmoid(SWIGLU_ALPHA * glu) * (lin + 1.0)).astype(BF16)
        y = jnp.dot(act, wdn_ref[...], preferred_element_type=F32) + bdn_ref[...]
        y_ref[...] = _pack_bf16_pair(y)


def _experts(xs, te, nact, w_up, b_up, w_down, b_down, tm):
    p_rows, dh = xs.shape
    ne, d, f2 = w_up.shape
    f = f2 // 2
    nt = p_rows // tm

    def row_map(t, te, nact):
        return (jnp.minimum(t, nact[0] - 1), 0)

    def w_map(t, te, nact):
        return (te[t], 0, 0)

    grid_spec = pltpu.PrefetchScalarGridSpec(
        num_scalar_prefetch=2,
        grid=(nt,),
        in_specs=[
            pl.BlockSpec((tm, dh), row_map),
            pl.BlockSpec((None, d, f2), w_map),
            pl.BlockSpec((None, 1, f2), w_map),
            pl.BlockSpec((None, f, d), w_map),
            pl.BlockSpec((None, 1, d), w_map),
        ],
        out_specs=pl.BlockSpec((tm, dh), lambda t, te, nact: (t, 0)),
    )
    return pl.pallas_call(
        _expert_kernel,
        out_shape=jax.ShapeDtypeStruct((p_rows, dh), U32),
        grid_spec=grid_spec,
        compiler_params=_cparams(("arbitrary",)),
        name="moe_experts",
    )(te, nact, xs, w_up, b_up.reshape(ne, 1, f2), w_down, b_down.reshape(ne, 1, d))


def _combine_kernel(pos_ref, posn_ref, wts_ref, s_ref, mod_ref, modn_ref, lng_ref, lnb_ref, ys_hbm,
                    snew_ref, *rest, alpha, k, with_next):
    if with_next:
        hn_ref, ybuf, sems = rest
    else:
        ybuf, sems = rest
    i = pl.program_id(0)
    n = pl.num_programs(0)
    tr = s_ref.shape[0]
    slot = lax.rem(i, 2)

    def issue(p_ref, sl):
        def body(it, c):
            for u in range(_ISSUE_UNROLL):
                r = it * _ISSUE_UNROLL + u
                for j in range(k):
                    pltpu.make_async_copy(
                        ys_hbm.at[pl.ds(p_ref[0, r * k + j], 1)], ybuf.at[sl, j, pl.ds(r, 1)], sems.at[sl]
                    ).start(priority=j % 2)
            return c

        lax.fori_loop(0, tr // _ISSUE_UNROLL, body, 0)

    @pl.when(i == 0)
    def _():
        issue(pos_ref, 0)

    @pl.when(i + 1 < n)
    def _():
        issue(posn_ref, 1 - slot)

    for j in range(k):
        pltpu.make_async_copy(ys_hbm.at[pl.ds(0, tr)], ybuf.at[slot, j], sems.at[slot]).wait()

    acc = None
    for j in range(k):
        lo, hi = _unpack_bf16_pair(ybuf[slot, j])
        yk = wts_ref[:, j:j + 1] * jnp.concatenate([lo, hi], axis=1)
        acc = yk if acc is None else acc + yk
    t = alpha * s_ref[...] + mod_ref[5:6, :] * acc
    sn = _ln(t) * lng_ref[...] + lnb_ref[...]
    snew_ref[...] = sn
    if with_next:
        hn_ref[...] = (_ln(sn) * (1.0 + modn_ref[1:2, :]) + modn_ref[0:1, :]).astype(hn_ref.dtype)


def _combine(ys, pos, wts, s2, mod, mod_next, ln_g, ln_b, seg_rows, ctx_len, alpha):
    m, d = s2.shape
    k = wts.shape[1]
    dh = ys.shape[1]
    tr = _tile(ctx_len, 256, _ISSUE_UNROLL)
    nt = m // tr
    spt = seg_rows // tr
    nct = ctx_len // tr
    pos3 = pos.reshape(nt, 1, tr * k)
    with_next = mod_next is not None
    row = pl.BlockSpec((1, d), lambda i: (0, 0))
    rows = pl.BlockSpec((tr, d), lambda i: (i, 0))
    modspec = pl.BlockSpec((None, None, N_MOD, d),
                           lambda i: (i // spt, jnp.minimum((i % spt) // nct, 1), 0, 0))
    out_shape = [jax.ShapeDtypeStruct((m, d), F32)]
    out_specs = [rows]
    if with_next:
        out_shape.append(jax.ShapeDtypeStruct((m, d), BF16))
        out_specs.append(rows)
    return pl.pallas_call(
        functools.partial(_combine_kernel, alpha=alpha, k=k, with_next=with_next),
        out_shape=tuple(out_shape),
        grid=(nt,),
        in_specs=[
            pl.BlockSpec((None, 1, tr * k), lambda i: (i, 0, 0), memory_space=pltpu.SMEM),
            pl.BlockSpec((None, 1, tr * k), lambda i: (jnp.minimum(i + 1, nt - 1), 0, 0),
                         memory_space=pltpu.SMEM),
            pl.BlockSpec((tr, k), lambda i: (i, 0)),
            rows, modspec, modspec, row, row,
            pl.BlockSpec(memory_space=pl.ANY),
        ],
        out_specs=tuple(out_specs),
        scratch_shapes=[pltpu.VMEM((2, k, tr, dh), U32), pltpu.SemaphoreType.DMA((2,))],
        compiler_params=_cparams(("arbitrary",)),
        name="moe_combine_residual",
    )(pos3, pos3, wts, s2, mod, mod_next if with_next else mod, ln_g, ln_b, ys)


def _to_column_major(t, ctx_len):
    b, s, d = t.shape
    lat = t[:, ctx_len:]
    rows = (s - ctx_len) // GRID_W
    lat = lat.reshape(b, rows, GRID_W, d).swapaxes(1, 2).reshape(b, s - ctx_len, d)
    return jnp.concatenate([t[:, :ctx_len], lat], axis=1)


def _to_row_major(t, ctx_len):
    b, s, d = t.shape
    lat = t[:, ctx_len:]
    rows = (s - ctx_len) // GRID_W
    lat = lat.reshape(b, GRID_W, rows, d).swapaxes(1, 2).reshape(b, s - ctx_len, d)
    return jnp.concatenate([t[:, :ctx_len], lat], axis=1)


def kernel(x, c, ctx, c_ctx, w_ada, b_ada, w_in, b_in, conv_w, conv_b, a_log_f, a_log_b, dt_bias_f, dt_bias_b, d_skip, ssm_norm_w, gmlp_ln_g, gmlp_ln_b, w_s, b_s, w_pa, w_pb, w_o, ln_g, ln_b, w_router, b_router, w_up, b_up, w_down, b_down):
    bsz, seq, d = x.shape
    ctx_len = ctx.shape[1]
    depth = w_ada.shape[0]
    s_tot = ctx_len + seq
    m_tot = bsz * s_tot
    nh = a_log_f.shape[1]
    ssm_w = ssm_norm_w.shape[1]
    p = ssm_w // nh
    xbc = conv_w.shape[2]
    n_state = (xbc - ssm_w) // (2 * SSM_GROUPS)
    gw = gmlp_ln_g.shape[1]
    ne = w_router.shape[2]
    assert 2 * nh <= LANES and ne <= LANES

    alpha = (2 * depth) ** 0.25
    off_xbc = ssm_w
    off_dtf = off_xbc + xbc
    off_u = off_dtf + 2 * nh
    off_v = off_u + gw
    off_ga = off_v + gw
    off_gb = off_ga + d
    m_off_u = off_dtf
    m_off_v = m_off_u + gw
    m_off_ga = m_off_v + gw
    m_off_gb = m_off_ga + d

    segs = ((0, ctx_len), (ctx_len, seq))
    tm_moe = 512 if (m_tot * TOP_K) % 512 == 0 else 128

    c_all = jnp.concatenate([c, c_ctx[None, :]], axis=0)
    mods = _ada(c_all, w_ada, b_ada).reshape(depth, bsz + 1, N_MOD, d)

    s_all = jnp.concatenate([ctx, x], axis=1)
    xs = jnp.zeros((m_tot * TOP_K + ne * tm_moe, d // 2), U32)

    def layer_mod(l):
        return jnp.stack(
            [jnp.broadcast_to(mods[l, bsz][None], (bsz, N_MOD, d)), mods[l, :bsz]], axis=1)

    h = _ln_mod(s_all, layer_mod(0), ctx_len, 0, 1)
    for l in range(depth):
        column_major = l % 2 == 1
        mod = layer_mod(l)

        w_main = jnp.concatenate([w_in[l][:, :off_dtf], w_in[l][:, off_u:]], axis=1).astype(BF16)
        b_main = jnp.concatenate([b_in[l][:off_dtf], b_in[l][off_u:]])[None, :]
        w_dt = jnp.pad(w_in[l][:, off_dtf:off_u], ((0, 0), (0, LANES - 2 * nh))).astype(BF16)
        b_dt = jnp.pad(b_in[l][off_dtf:off_u], (0, LANES - 2 * nh))[None, :]

        if column_major:
            h = _to_column_major(h, ctx_len)
        h2 = h.reshape(m_tot, d)
        p_main = _matmul(h2, w_main, b_main, BF16, "in_proj")
        dtraw = _matmul(h2, w_dt, b_dt, F32, "dt_proj").reshape(bsz, s_tot, LANES)
        p3 = p_main.reshape(bsz, s_tot, -1)

        xact = _conv_silu(p3, conv_w[l], conv_b[l], off_xbc, segs)
        alog_row = jnp.pad(jnp.concatenate([a_log_f[l], a_log_b[l]]), (0, LANES - 2 * nh))[None, :]
        dtb_row = jnp.pad(jnp.concatenate([dt_bias_f[l], dt_bias_b[l]]), (0, LANES - 2 * nh))[None, :]
        yf, yb = _ssd(xact, dtraw, alog_row, dtb_row, nh, p, n_state, ctx_len)
        d_exp = jnp.repeat(d_skip[l], p)[None, :]
        g_out = _gmlp(p3, m_off_u, m_off_v, gmlp_ln_g[l], gmlp_ln_b[l], w_s[l], b_s[l])
        merged = _merge(yf.reshape(m_tot, ssm_w), yb.reshape(m_tot, ssm_w), xact.reshape(m_tot, xbc), p_main,
                        d_exp, ssm_norm_w[l][None, :], g_out.reshape(m_tot, gw),
                        m_off_ga, m_off_gb, w_pa[l].astype(BF16), w_pb[l].astype(BF16))
        merged = merged.reshape(bsz, s_tot, d)
        if column_major:
            merged = _to_row_major(merged, ctx_len)

        w_r = jnp.pad(w_router[l], ((0, 0), (0, LANES - ne))).astype(BF16)
        b_r = jnp.pad(b_router[l], (0, LANES - ne), constant_values=MASK_NEG)[None, :]
        s_all, h_moe, idx, wts = _wo_residual(
            merged, w_o[l].astype(BF16), s_all, mod, ln_g[l, 0][None, :], ln_b[l, 0][None, :],
            w_r, b_r, ctx_len, alpha)

        pos, te, nact = _moe_plan(idx.reshape(m_tot, TOP_K), ne, tm_moe)
        xs = _dispatch(h_moe.reshape(m_tot, d // 2), pos, xs)
        ys = _experts(xs, te, nact, w_up[l].astype(BF16), b_up[l], w_down[l].astype(BF16), b_down[l], tm_moe)
        outs = _combine(ys, pos, wts.reshape(m_tot, TOP_K), s_all.reshape(m_tot, d), mod,
                        layer_mod(l + 1) if l + 1 < depth else None,
                        ln_g[l, 1][None, :], ln_b[l, 1][None, :], s_tot, ctx_len, alpha)
        s_all = outs[0].reshape(bsz, s_tot, d)
        if l + 1 < depth:
            h = outs[1].reshape(bsz, s_tot, d)

    return s_all[:, ctx_len:]
```

```python
import functools

import jax
import jax.numpy as jnp
from jax import lax
from jax.experimental import pallas as pl
from jax.experimental.pallas import tpu as pltpu

F32 = jnp.float32
BF16 = jnp.bfloat16
U32 = jnp.uint32
I32 = jnp.int32

SSM_GROUPS = 8
SSM_CHUNK = 128
TOP_K = 4
GRID_W = 64
N_MOD = 6
SWIGLU_LIMIT = 7.0
SWIGLU_ALPHA = 1.702
LN_EPS = 1e-5

LANES = 128
VMEM_LIMIT = 56 * 1024 * 1024
MASK_NEG = -1e30


def _cparams(sem):
    return pltpu.CompilerParams(dimension_semantics=sem, vmem_limit_bytes=VMEM_LIMIT)


def _tile(n, pref, mult=8):
    if n <= pref:
        return n
    t = (pref // mult) * mult
    while t >= mult:
        if n % t == 0:
            return t
        t -= mult
    raise ValueError(f"no tile for {n} <= {pref}")


def _ln(x):
    mu = jnp.mean(x, axis=-1, keepdims=True)
    xc = x - mu
    var = jnp.mean(xc * xc, axis=-1, keepdims=True)
    return xc * lax.rsqrt(var + LN_EPS)


def _sigmoid(x):
    return 1.0 / (1.0 + jnp.exp(-x))


def _gelu_tanh(x):
    c = 0.7978845608028654
    return 0.5 * x * (1.0 + jnp.tanh(c * (x + 0.044715 * (x * x * x))))


def _softplus(x):
    return jnp.maximum(x, 0.0) + jnp.log(1.0 + jnp.exp(-jnp.abs(x)))


def _pack_bf16_pair(y):
    dh = y.shape[-1] // 2
    lo = lax.bitcast_convert_type(y[:, :dh].astype(BF16).astype(F32), U32)
    hi = lax.bitcast_convert_type(y[:, dh:].astype(BF16).astype(F32), U32)
    return (hi & jnp.uint32(0xFFFF0000)) | (lo >> 16)


def _unpack_bf16_pair(w):
    lo = lax.bitcast_convert_type(w << 16, F32)
    hi = lax.bitcast_convert_type(w & jnp.uint32(0xFFFF0000), F32)
    return lo, hi


def _ada_kernel(c_ref, w_ref, b_ref, o_ref):
    c = c_ref[...]
    a = (c * _sigmoid(c)).astype(BF16)
    o_ref[...] = jnp.dot(a, w_ref[...].astype(BF16), preferred_element_type=F32) + b_ref[...]


def _ada(c_all, w_ada, b_ada):
    depth, d, n = w_ada.shape
    r = c_all.shape[0]
    tn = _tile(n, 1024, LANES)
    return pl.pallas_call(
        _ada_kernel,
        out_shape=jax.ShapeDtypeStruct((depth, r, n), F32),
        grid=(depth, n // tn),
        in_specs=[
            pl.BlockSpec((r, d), lambda l, j: (0, 0)),
            pl.BlockSpec((None, d, tn), lambda l, j: (l, 0, j)),
            pl.BlockSpec((None, 1, tn), lambda l, j: (l, 0, j)),
        ],
        out_specs=pl.BlockSpec((None, r, tn), lambda l, j: (l, 0, j)),
        compiler_params=_cparams(("parallel", "parallel")),
        name="ada_mod",
    )(c_all, w_ada, b_ada.reshape(depth, 1, n))


def _ln_mod_kernel(s_ref, mod_ref, h_ref, *, shift_row, scale_row):
    y = _ln(s_ref[...])
    scale = mod_ref[scale_row:scale_row + 1, :]
    shift = mod_ref[shift_row:shift_row + 1, :]
    h_ref[...] = (y * (1.0 + scale) + shift).astype(h_ref.dtype)


def _ln_mod(s_all, mod, ctx_len, shift_row, scale_row):
    b, s, d = s_all.shape
    tr = _tile(ctx_len, 256)
    nct = ctx_len // tr
    return pl.pallas_call(
        functools.partial(_ln_mod_kernel, shift_row=shift_row, scale_row=scale_row),
        out_shape=jax.ShapeDtypeStruct((b, s, d), BF16),
        grid=(b, s // tr),
        in_specs=[
            pl.BlockSpec((None, tr, d), lambda bi, i: (bi, i, 0)),
            pl.BlockSpec((None, None, N_MOD, d), lambda bi, i: (bi, jnp.minimum(i // nct, 1), 0, 0)),
        ],
        out_specs=pl.BlockSpec((None, tr, d), lambda bi, i: (bi, i, 0)),
        compiler_params=_cparams(("parallel", "parallel")),
        name="ln_mod",
    )(s_all, mod)


def _mm_kernel(a_ref, w_ref, b_ref, o_ref):
    acc = jnp.dot(a_ref[...], w_ref[...], preferred_element_type=F32)
    o_ref[...] = (acc + b_ref[...]).astype(o_ref.dtype)


def _matmul(a, w, bias, out_dtype, name):
    m, k = a.shape
    n = w.shape[1]
    tm = _tile(m, 1024)
    tn = _tile(n, 1024, LANES)
    return pl.pallas_call(
        _mm_kernel,
        out_shape=jax.ShapeDtypeStruct((m, n), out_dtype),
        grid=(n // tn, m // tm),
        in_specs=[
            pl.BlockSpec((tm, k), lambda j, i: (i, 0)),
            pl.BlockSpec((k, tn), lambda j, i: (0, j)),
            pl.BlockSpec((1, tn), lambda j, i: (0, j)),
        ],
        out_specs=pl.BlockSpec((tm, tn), lambda j, i: (i, j)),
        compiler_params=_cparams(("parallel", "parallel")),
        name=name,
    )(a, w, bias)


_CONV_ROWS = 128
_CONV_WIN = 256


def _conv_kernel(x_ref, w_ref, b_ref, o_ref, *, segs, kw):
    half = kw // 2
    rb = _CONV_ROWS
    shift_mats = {}

    taps = [k for k in range(kw) if k != half]

    def shifts(win, off):
        if (win, off) not in shift_mats:
            t = lax.broadcasted_iota(I32, (rb, win), 0)
            j = lax.broadcasted_iota(I32, (rb, win), 1)
            shift_mats[(win, off)] = jnp.concatenate(
                [jnp.where(j == t + (off + k - half), 1.0, 0.0).astype(BF16) for k in taps], axis=0)
        return shift_mats[(win, off)]

    for st, ln in segs:
        win = min(_CONV_WIN, ln)
        for r0 in range(st, st + ln, rb):
            lo = min(max(r0 - (win - rb) // 2, st), st + ln - win)
            shifted = jnp.dot(shifts(win, r0 - lo), x_ref[lo:lo + win, :], preferred_element_type=F32)
            acc = b_ref[...] + w_ref[half:half + 1, :] * x_ref[r0:r0 + rb, :].astype(F32)
            for n, k in enumerate(taps):
                acc = acc + w_ref[k:k + 1, :] * shifted[n * rb:(n + 1) * rb, :]
            o_ref[r0:r0 + rb, :] = (acc * _sigmoid(acc)).astype(o_ref.dtype)


def _conv_silu(p_main, conv_w, conv_b, col_off, segs):
    b, s, _ = p_main.shape
    kw, xbc = conv_w.shape
    cw = 512 if (xbc % 512 == 0 and col_off % 512 == 0) else LANES
    assert xbc % cw == 0 and col_off % cw == 0
    assert all(ln % _CONV_ROWS == 0 and (ln >= _CONV_WIN or ln == _CONV_ROWS) for _, ln in segs)
    assert kw // 2 <= (_CONV_WIN - _CONV_ROWS) // 2
    ob = col_off // cw
    return pl.pallas_call(
        functools.partial(_conv_kernel, segs=segs, kw=kw),
        out_shape=jax.ShapeDtypeStruct((b, s, xbc), BF16),
        grid=(b, xbc // cw),
        in_specs=[
            pl.BlockSpec((None, s, cw), lambda bi, j: (bi, 0, ob + j)),
            pl.BlockSpec((kw, cw), lambda bi, j: (0, j)),
            pl.BlockSpec((1, cw), lambda bi, j: (0, j)),
        ],
        out_specs=pl.BlockSpec((None, s, cw), lambda bi, j: (bi, 0, j)),
        compiler_params=_cparams(("parallel", "parallel")),
        name="conv_silu",
    )(p_main, conv_w, conv_b.reshape(1, xbc))


def _head_rows(v_t, c0, hpg, p):
    q = v_t.shape[1]
    return jnp.concatenate(
        [jnp.broadcast_to(v_t[c0 + h:c0 + h + 1, :], (p, q)) for h in range(hpg)], axis=0)


def _ssd_dir(xa_ref, dtr_ref, arow, dtb_row, y_ref, st_ref, *, backward, nh, ng, p, n, q):
    hpg = nh // ng
    width = nh * p
    gw = hpg * p
    col0 = nh if backward else 0
    dt = _softplus(dtr_ref[...] + dtb_row)
    da = dt * arow
    ri = lax.broadcasted_iota(I32, (q, q), 0)
    ci = lax.broadcasted_iota(I32, (q, q), 1)
    tri = jnp.where((ri <= ci) if backward else (ri >= ci), 1.0, 0.0).astype(F32)
    cs = jnp.dot(tri, da, preferred_element_type=F32, precision=lax.Precision.HIGHEST)
    tot = cs[0:1, :] if backward else cs[q - 1:q, :]
    cdec = jnp.exp(tot)
    dt_t = dt.T
    cs_t = cs.T
    wts_t = (dt * jnp.exp(tot - cs)).T
    ecs_t = jnp.exp(cs).T
    mask_t = (ci <= ri) if backward else (ci >= ri)
    eye_q = jnp.where(ri == ci, 1.0, 0.0).astype(BF16)
    eye_w = jnp.where(lax.broadcasted_iota(I32, (gw, gw), 0) == lax.broadcasted_iota(I32, (gw, gw), 1),
                      1.0, 0.0).astype(BF16)
    nt_dims = (((1,), (1,)), ((), ()))
    for g in range(ng):
        x_t = xa_ref[:, g * gw:(g + 1) * gw].T.astype(F32)
        bg = xa_ref[:, width + g * n:width + (g + 1) * n]
        cg = xa_ref[:, width + ng * n + g * n:width + ng * n + (g + 1) * n]
        cb_t = lax.dot_general(bg, cg, nt_dims, preferred_element_type=F32)
        c0 = col0 + g * hpg
        yd = []
        for h in range(hpg):
            c = c0 + h
            seg = cs_t[c:c + 1, :] - cs[:, c:c + 1]
            m_t = (cb_t * jnp.exp(jnp.where(mask_t, seg, MASK_NEG))).astype(BF16)
            xdt_t = (x_t[h * p:(h + 1) * p, :] * dt_t[c:c + 1, :]).astype(BF16)
            yd.append(jnp.dot(xdt_t, m_t, preferred_element_type=F32))
        st_old = st_ref[g]
        yo_t = lax.dot_general(st_old.astype(BF16), cg, nt_dims, preferred_element_type=F32)
        y_t = jnp.concatenate(yd, axis=0) + yo_t * _head_rows(ecs_t, c0, hpg, p)
        xw_t = (x_t * _head_rows(wts_t, c0, hpg, p)).astype(BF16)
        cdec_rows = jnp.concatenate(
            [jnp.broadcast_to(cdec[:, c0 + h:c0 + h + 1], (p, n)) for h in range(hpg)], axis=0)
        st_ref[g] = st_old * cdec_rows + jnp.dot(xw_t, bg, preferred_element_type=F32)
        y_ref[:, g * gw:(g + 1) * gw] = y_t.T.astype(y_ref.dtype)


def _ssd_kernel(xf_ref, xb_ref, dtf_ref, dtb_ref, alog_ref, dtbias_ref, yf_ref, yb_ref,
                sf_ref, sb_ref, **kw):
    @pl.when(pl.program_id(1) == 0)
    def _():
        sf_ref[...] = jnp.zeros_like(sf_ref)
        sb_ref[...] = jnp.zeros_like(sb_ref)

    arow = -jnp.exp(alog_ref[...])
    dtb_row = dtbias_ref[...]
    _ssd_dir(xf_ref, dtf_ref, arow, dtb_row, yf_ref, sf_ref, backward=False, **kw)
    _ssd_dir(xb_ref, dtb_ref, arow, dtb_row, yb_ref, sb_ref, backward=True, **kw)


def _ssd(xact, dtraw, alog_row, dtbias_row, nh, p, n, ctx_len):
    b, s, xbc = xact.shape
    q = SSM_CHUNK
    ng = SSM_GROUPS
    width = nh * p
    ncc = ctx_len // q
    nc = s // q
    ncl = nc - ncc

    def fwd(bi, i):
        return (bi, i, 0)

    def bwd(bi, i):
        return (bi, jnp.where(i < ncc, ncc - 1 - i, 2 * ncc + ncl - 1 - i), 0)

    kern = functools.partial(_ssd_kernel, nh=nh, ng=ng, p=p, n=n, q=q)
    return pl.pallas_call(
        kern,
        out_shape=(jax.ShapeDtypeStruct((b, s, width), BF16), jax.ShapeDtypeStruct((b, s, width), BF16)),
        grid=(b, nc),
        in_specs=[
            pl.BlockSpec((None, q, xbc), fwd),
            pl.BlockSpec((None, q, xbc), bwd),
            pl.BlockSpec((None, q, LANES), fwd),
            pl.BlockSpec((None, q, LANES), bwd),
            pl.BlockSpec((1, LANES), lambda bi, i: (0, 0)),
            pl.BlockSpec((1, LANES), lambda bi, i: (0, 0)),
        ],
        out_specs=(pl.BlockSpec((None, q, width), fwd), pl.BlockSpec((None, q, width), bwd)),
        scratch_shapes=[pltpu.VMEM((ng, (nh // ng) * p, n), F32), pltpu.VMEM((ng, (nh // ng) * p, n), F32)],
        compiler_params=_cparams(("parallel", "arbitrary")),
        name="ssd_scan",
    )(xact, xact, dtraw, dtraw, alog_row, dtbias_row)


def _gnorm_kernel(yf_ref, yb_ref, x_ref, z_ref, d_ref, w_ref, o_ref, *, ng):
    x = x_ref[...].astype(F32)
    z = z_ref[...].astype(F32)
    y = yf_ref[...].astype(F32) + yb_ref[...].astype(F32) + d_ref[...] * x
    g = y * (z * _sigmoid(z))
    gw = g.shape[-1] // ng
    for k in range(ng):
        gk = g[:, k * gw:(k + 1) * gw]
        ms = jnp.mean(gk * gk, axis=-1, keepdims=True)
        o_ref[:, k * gw:(k + 1) * gw] = (
            gk * lax.rsqrt(ms + LN_EPS) * w_ref[:, k * gw:(k + 1) * gw]).astype(o_ref.dtype)


def _gated_rmsnorm(yf, yb, xact2d, p_main2d, d_exp, norm_w):
    m, width = yf.shape
    tr = _tile(m, 256)
    blk = pl.BlockSpec((tr, width), lambda i: (i, 0))
    row = pl.BlockSpec((1, width), lambda i: (0, 0))
    return pl.pallas_call(
        functools.partial(_gnorm_kernel, ng=SSM_GROUPS),
        out_shape=jax.ShapeDtypeStruct((m, width), BF16),
        grid=(m // tr,),
        in_specs=[blk, blk, blk, blk, row, row],
        out_specs=blk,
        compiler_params=_cparams(("parallel",)),
        name="gated_rmsnorm",
    )(yf, yb, xact2d, p_main2d, d_exp, norm_w)


def _gmlp_kernel(u_ref, v_ref, lng_ref, lnb_ref, ws_ref, bst_ref, o_ref, *, ng):
    v = _gelu_tanh(v_ref[...].astype(F32))
    vb = (_ln(v) * lng_ref[...] + lnb_ref[...]).astype(BF16)
    u = _gelu_tanh(u_ref[...].astype(F32))
    gw = u.shape[-1] // ng
    for g in range(ng):
        mixed = jnp.dot(ws_ref[g], vb[:, g * gw:(g + 1) * gw], preferred_element_type=F32)
        mixed = mixed + bst_ref[:, g:g + 1]
        o_ref[:, g * gw:(g + 1) * gw] = (u[:, g * gw:(g + 1) * gw] * mixed).astype(o_ref.dtype)


def _gmlp(p_main, off_u, off_v, ln_g, ln_b, w_s, b_s):
    b, s, _ = p_main.shape
    ng, q, _ = w_s.shape
    wg = ln_g.shape[-1]
    assert off_u % wg == 0 and off_v % wg == 0
    bu, bv = off_u // wg, off_v // wg
    row = pl.BlockSpec((1, wg), lambda bi, c: (0, 0))
    return pl.pallas_call(
        functools.partial(_gmlp_kernel, ng=ng),
        out_shape=jax.ShapeDtypeStruct((b, s, wg), BF16),
        grid=(b, s // q),
        in_specs=[
            pl.BlockSpec((None, q, wg), lambda bi, c: (bi, c, bu)),
            pl.BlockSpec((None, q, wg), lambda bi, c: (bi, c, bv)),
            row, row,
            pl.BlockSpec((ng, q, q), lambda bi, c: (0, 0, 0)),
            pl.BlockSpec((q, ng), lambda bi, c: (0, 0)),
        ],
        out_specs=pl.BlockSpec((None, q, wg), lambda bi, c: (bi, c, 0)),
        compiler_params=_cparams(("parallel", "parallel")),
        name="gmlp",
    )(p_main, p_main, ln_g.reshape(1, wg), ln_b.reshape(1, wg), w_s.astype(BF16), b_s.T)


def _merge_kernel(a_ref, g_ref, ga_ref, gb_ref, wpa_ref, wpb_ref, o_ref):
    pa = jnp.dot(a_ref[...], wpa_ref[...], preferred_element_type=F32)
    pb = jnp.dot(g_ref[...], wpb_ref[...], preferred_element_type=F32)
    m = _sigmoid(ga_ref[...].astype(F32)) * pa + _sigmoid(gb_ref[...].astype(F32)) * pb
    o_ref[...] = m.astype(o_ref.dtype)


def _merge(ssd_out, g_out, p_main2d, off_ga, off_gb, w_pa, w_pb):
    m, wa = ssd_out.shape
    wb = g_out.shape[1]
    n = w_pa.shape[1]
    tm = _tile(m, 512)
    tn = _tile(n, 1024, LANES)
    assert off_ga % tn == 0 and off_gb % tn == 0
    ba, bb = off_ga // tn, off_gb // tn
    return pl.pallas_call(
        _merge_kernel,
        out_shape=jax.ShapeDtypeStruct((m, n), BF16),
        grid=(n // tn, m // tm),
        in_specs=[
            pl.BlockSpec((tm, wa), lambda j, i: (i, 0)),
            pl.BlockSpec((tm, wb), lambda j, i: (i, 0)),
            pl.BlockSpec((tm, tn), lambda j, i: (i, ba + j)),
            pl.BlockSpec((tm, tn), lambda j, i: (i, bb + j)),
            pl.BlockSpec((wa, tn), lambda j, i: (0, j)),
            pl.BlockSpec((wb, tn), lambda j, i: (0, j)),
        ],
        out_specs=pl.BlockSpec((tm, tn), lambda j, i: (i, j)),
        compiler_params=_cparams(("parallel", "parallel")),
        name="merge_branches",
    )(ssd_out, g_out, p_main2d, p_main2d, w_pa, w_pb)


def _top_k(logits, k):
    lane = lax.broadcasted_iota(I32, logits.shape, 1)
    vals, idxs = [], []
    cur = logits
    for _ in range(k):
        m = jnp.max(cur, axis=-1, keepdims=True)
        idx = jnp.min(jnp.where(cur == m, lane, LANES), axis=-1, keepdims=True)
        vals.append(m)
        idxs.append(idx)
        cur = jnp.where(lane == idx, -jnp.inf, cur)
    return vals, idxs


def _wo_res_kernel(m_ref, wo_ref, s_ref, modc_ref, modl_ref, lng_ref, lnb_ref, wr_ref, br_ref,
                   snew_ref, h_ref, idx_ref, wts_ref, *, alpha, k, sub, nctx):
    y = jnp.dot(m_ref[...], wo_ref[...], preferred_element_type=F32)
    first = pl.program_id(1) == 0
    for r in range(m_ref.shape[0] // sub):
        rows = slice(r * sub, (r + 1) * sub)
        mod = jnp.where(first, modc_ref[...], modl_ref[...]) if r < nctx else modl_ref[...]
        t = alpha * s_ref[rows, :] + mod[2:3, :] * y[rows, :]
        sn = _ln(t) * lng_ref[...] + lnb_ref[...]
        snew_ref[rows, :] = sn
        h = _ln(sn) * (1.0 + mod[4:5, :]) + mod[3:4, :]
        h_ref[rows, :] = _pack_bf16_pair(h)
        logits = jnp.dot(h.astype(BF16), wr_ref[...], preferred_element_type=F32) + br_ref[...]
        vals, idxs = _top_k(logits, k)
        es = [jnp.exp(v - vals[0]) for v in vals]
        den = es[0]
        for e in es[1:]:
            den = den + e
        for j in range(k):
            idx_ref[rows, j:j + 1] = idxs[j]
            wts_ref[rows, j:j + 1] = es[j] / den


_WO_ROWS = 768


def _wo_residual(m_all, w_o, s_all, mod, ln_g, ln_b, w_router, b_router, ctx_len, alpha):
    b, s, d = s_all.shape
    sub = _tile(ctx_len, 256)
    nctx = ctx_len // sub
    tr = sub * max(t for t in range(1, s // sub + 1)
                   if (s // sub) % t == 0 and t >= nctx and t * sub <= max(_WO_ROWS, ctx_len))
    blk = lambda w: pl.BlockSpec((None, tr, w), lambda bi, i: (bi, i, 0))
    row = pl.BlockSpec((1, d), lambda bi, i: (0, 0))
    once = dict(pipeline_mode=pl.Buffered(1))
    return pl.pallas_call(
        functools.partial(_wo_res_kernel, alpha=alpha, k=TOP_K, sub=sub, nctx=nctx),
        out_shape=(
            jax.ShapeDtypeStruct((b, s, d), F32),
            jax.ShapeDtypeStruct((b, s, d // 2), U32),
            jax.ShapeDtypeStruct((b, s, TOP_K), I32),
            jax.ShapeDtypeStruct((b, s, TOP_K), F32),
        ),
        grid=(b, s // tr),
        in_specs=[
            blk(d),
            pl.BlockSpec((d, d), lambda bi, i: (0, 0), **once),
            blk(d),
            pl.BlockSpec((None, None, N_MOD, d), lambda bi, i: (bi, 0, 0, 0)),
            pl.BlockSpec((None, None, N_MOD, d), lambda bi, i: (bi, 1, 0, 0)),
            row, row,
            pl.BlockSpec((d, LANES), lambda bi, i: (0, 0), **once),
            pl.BlockSpec((1, LANES), lambda bi, i: (0, 0)),
        ],
        out_specs=(blk(d), blk(d // 2), blk(TOP_K), blk(TOP_K)),
        compiler_params=_cparams(("parallel", "arbitrary")),
        name="wo_residual_router",
    )(m_all, w_o, s_all, mod, mod, ln_g, ln_b, w_router, b_router)


def _moe_plan(idx, ne, tm):
    m, k = idx.shape
    na = m * k
    assert na % tm == 0
    e = idx.reshape(na)
    onehot = (e[:, None] == jnp.arange(ne, dtype=I32)[None, :]).astype(I32)
    csum = jnp.cumsum(onehot, axis=0)
    rank = jnp.sum(csum * onehot, axis=1) - 1
    counts = csum[-1]
    ptiles = (counts + tm - 1) // tm
    tile_end = jnp.cumsum(ptiles)
    tile_start = tile_end - ptiles
    pos = (jnp.sum(onehot * tile_start[None, :], axis=1) * tm + rank).astype(I32)
    nt = na // tm + ne
    tids = jnp.arange(nt, dtype=I32)
    te = jnp.minimum(jnp.sum((tids[:, None] >= tile_end[None, :]).astype(I32), axis=1), ne - 1)
    return pos, te.astype(I32), tile_end[-1:].astype(I32)


_ISSUE_UNROLL = 8


def _dispatch_kernel(pos_ref, h_ref, xs_in, xs_out, sem, *, k):
    del xs_in
    tr = h_ref.shape[0]

    def issue(i, c):
        for u in range(_ISSUE_UNROLL):
            r = i * _ISSUE_UNROLL + u
            for j in range(k):
                pltpu.make_async_copy(
                    h_ref.at[pl.ds(r, 1)], xs_out.at[pl.ds(pos_ref[0, r * k + j], 1)], sem
                ).start(priority=j % 2)
        return c

    lax.fori_loop(0, tr // _ISSUE_UNROLL, issue, 0)
    for j in range(k):
        pltpu.make_async_copy(h_ref, xs_out.at[pl.ds(0, tr)], sem).wait()


def _dispatch(h_packed, pos, xs):
    m, dh = h_packed.shape
    k = pos.shape[0] // m
    tr = _tile(m, 256, _ISSUE_UNROLL)
    return pl.pallas_call(
        functools.partial(_dispatch_kernel, k=k),
        out_shape=jax.ShapeDtypeStruct(xs.shape, xs.dtype),
        grid=(m // tr,),
        in_specs=[
            pl.BlockSpec((None, 1, tr * k), lambda i: (i, 0, 0), memory_space=pltpu.SMEM),
            pl.BlockSpec((tr, dh), lambda i: (i, 0)),
            pl.BlockSpec(memory_space=pl.ANY),
        ],
        out_specs=pl.BlockSpec(memory_space=pl.ANY),
        scratch_shapes=[pltpu.SemaphoreType.DMA(())],
        input_output_aliases={2: 0},
        compiler_params=_cparams(("arbitrary",)),
        name="moe_dispatch",
    )(pos.reshape(m // tr, 1, tr * k), h_packed, xs)


def _expert_kernel(te_ref, nact_ref, x_ref, wup_ref, bup_ref, wdn_ref, bdn_ref, y_ref, wup_bf, wdn_bf):
    t = pl.program_id(0)
    active = t < nact_ref[0]

    @pl.when(jnp.logical_not(active))
    def _():
        y_ref[...] = jnp.zeros_like(y_ref)

    @pl.when(jnp.logical_and(active, jnp.logical_or(t == 0, te_ref[t] != te_ref[jnp.maximum(t - 1, 0)])))
    def _():
        wup_bf[...] = wup_ref[...].astype(BF16)
        wdn_bf[...] = wdn_ref[...].astype(BF16)

    @pl.when(active)
    def _():
        dh = x_ref.shape[-1]
        lo, hi = _unpack_bf16_pair(x_ref[...])
        hid = jnp.dot(lo.astype(BF16), wup_bf[0:dh, :], preferred_element_type=F32)
        hid = hid + jnp.dot(hi.astype(BF16), wup_bf[dh:, :], preferred_element_type=F32)
        hid = hid + bup_ref[...]
        f = hid.shape[-1] // 2
        glu = jnp.minimum(hid[:, :f], SWIGLU_LIMIT)
        lin = jnp.clip(hid[:, f:], -SWIGLU_LIMIT, SWIGLU_LIMIT)
        act = (glu * _sigmoid(SWIGLU_ALPHA * glu) * (lin + 1.0)).astype(BF16)
        y = jnp.dot(act, wdn_bf[...], preferred_element_type=F32) + bdn_ref[...]
        y_ref[...] = _pack_bf16_pair(y)


def _experts(xs, te, nact, w_up, b_up, w_down, b_down, tm):
    p_rows, dh = xs.shape
    ne, d, f2 = w_up.shape
    f = f2 // 2
    nt = p_rows // tm

    def row_map(t, te, nact):
        return (jnp.minimum(t, nact[0] - 1), 0)

    def w_map(t, te, nact):
        return (te[t], 0, 0)

    grid_spec = pltpu.PrefetchScalarGridSpec(
        num_scalar_prefetch=2,
        grid=(nt,),
        in_specs=[
            pl.BlockSpec((tm, dh), row_map),
            pl.BlockSpec((None, d, f2), w_map),
            pl.BlockSpec((None, 1, f2), w_map),
            pl.BlockSpec((None, f, d), w_map),
            pl.BlockSpec((None, 1, d), w_map),
        ],
        out_specs=pl.BlockSpec((tm, dh), lambda t, te, nact: (t, 0)),
        scratch_shapes=[pltpu.VMEM((d, f2), BF16), pltpu.VMEM((f, d), BF16)],
    )
    return pl.pallas_call(
        _expert_kernel,
        out_shape=jax.ShapeDtypeStruct((p_rows, dh), U32),
        grid_spec=grid_spec,
        compiler_params=_cparams(("arbitrary",)),
        name="moe_experts",
    )(te, nact, xs, w_up, b_up.reshape(ne, 1, f2), w_down, b_down.reshape(ne, 1, d))


def _combine_kernel(pos_ref, posn_ref, wts_ref, s_ref, mod_ref, modn_ref, lng_ref, lnb_ref, ys_hbm,
                    snew_ref, *rest, alpha, k, with_next):
    if with_next:
        hn_ref, ybuf, sems = rest
    else:
        ybuf, sems = rest
    i = pl.program_id(0)
    n = pl.num_programs(0)
    tr = s_ref.shape[0]
    slot = lax.rem(i, 2)

    def issue(p_ref, sl):
        def body(it, c):
            for u in range(_ISSUE_UNROLL):
                r = it * _ISSUE_UNROLL + u
                for j in range(k):
                    pltpu.make_async_copy(
                        ys_hbm.at[pl.ds(p_ref[0, r * k + j], 1)], ybuf.at[sl, j, pl.ds(r, 1)], sems.at[sl]
                    ).start(priority=j % 2)
            return c

        lax.fori_loop(0, tr // _ISSUE_UNROLL, body, 0)

    @pl.when(i == 0)
    def _():
        issue(pos_ref, 0)

    for sl in range(2):
        @pl.when(jnp.logical_and(i + 1 < n, slot == 1 - sl))
        def _():
            issue(posn_ref, sl)

    for j in range(k):
        pltpu.make_async_copy(ys_hbm.at[pl.ds(0, tr)], ybuf.at[slot, j], sems.at[slot]).wait()

    acc = None
    for j in range(k):
        lo, hi = _unpack_bf16_pair(ybuf[slot, j])
        yk = wts_ref[:, j:j + 1] * jnp.concatenate([lo, hi], axis=1)
        acc = yk if acc is None else acc + yk
    t = alpha * s_ref[...] + mod_ref[5:6, :] * acc
    sn = _ln(t) * lng_ref[...] + lnb_ref[...]
    snew_ref[...] = sn
    if with_next:
        hn_ref[...] = (_ln(sn) * (1.0 + modn_ref[1:2, :]) + modn_ref[0:1, :]).astype(hn_ref.dtype)


def _combine(ys, pos, wts, s2, mod, mod_next, ln_g, ln_b, seg_rows, ctx_len, alpha):
    m, d = s2.shape
    k = wts.shape[1]
    dh = ys.shape[1]
    tr = _tile(ctx_len, 256, _ISSUE_UNROLL)
    nt = m // tr
    spt = seg_rows // tr
    nct = ctx_len // tr
    pos3 = pos.reshape(nt, 1, tr * k)
    with_next = mod_next is not None
    row = pl.BlockSpec((1, d), lambda i: (0, 0))
    rows = pl.BlockSpec((tr, d), lambda i: (i, 0))
    modspec = pl.BlockSpec((None, None, N_MOD, d),
                           lambda i: (i // spt, jnp.minimum((i % spt) // nct, 1), 0, 0))
    out_shape = [jax.ShapeDtypeStruct((m, d), F32)]
    out_specs = [rows]
    if with_next:
        out_shape.append(jax.ShapeDtypeStruct((m, d), BF16))
        out_specs.append(rows)
    return pl.pallas_call(
        functools.partial(_combine_kernel, alpha=alpha, k=k, with_next=with_next),
        out_shape=tuple(out_shape),
        grid=(nt,),
        in_specs=[
            pl.BlockSpec((None, 1, tr * k), lambda i: (i, 0, 0), memory_space=pltpu.SMEM),
            pl.BlockSpec((None, 1, tr * k), lambda i: (jnp.minimum(i + 1, nt - 1), 0, 0),
                         memory_space=pltpu.SMEM),
            pl.BlockSpec((tr, k), lambda i: (i, 0)),
            rows, modspec, modspec, row, row,
            pl.BlockSpec(memory_space=pl.ANY),
        ],
        out_specs=tuple(out_specs),
        scratch_shapes=[pltpu.VMEM((2, k, tr, dh), U32), pltpu.SemaphoreType.DMA((2,))],
        compiler_params=_cparams(("arbitrary",)),
        name="moe_combine_residual",
    )(pos3, pos3, wts, s2, mod, mod_next if with_next else mod, ln_g, ln_b, ys)


def _to_column_major(t, ctx_len):
    b, s, d = t.shape
    lat = t[:, ctx_len:]
    rows = (s - ctx_len) // GRID_W
    lat = lat.reshape(b, rows, GRID_W, d).swapaxes(1, 2).reshape(b, s - ctx_len, d)
    return jnp.concatenate([t[:, :ctx_len], lat], axis=1)


def _to_row_major(t, ctx_len):
    b, s, d = t.shape
    lat = t[:, ctx_len:]
    rows = (s - ctx_len) // GRID_W
    lat = lat.reshape(b, GRID_W, rows, d).swapaxes(1, 2).reshape(b, s - ctx_len, d)
    return jnp.concatenate([t[:, :ctx_len], lat], axis=1)


def kernel(x, c, ctx, c_ctx, w_ada, b_ada, w_in, b_in, conv_w, conv_b, a_log_f, a_log_b, dt_bias_f, dt_bias_b, d_skip, ssm_norm_w, gmlp_ln_g, gmlp_ln_b, w_s, b_s, w_pa, w_pb, w_o, ln_g, ln_b, w_router, b_router, w_up, b_up, w_down, b_down):
    bsz, seq, d = x.shape
    ctx_len = ctx.shape[1]
    depth = w_ada.shape[0]
    s_tot = ctx_len + seq
    m_tot = bsz * s_tot
    nh = a_log_f.shape[1]
    ssm_w = ssm_norm_w.shape[1]
    p = ssm_w // nh
    xbc = conv_w.shape[2]
    n_state = (xbc - ssm_w) // (2 * SSM_GROUPS)
    gw = gmlp_ln_g.shape[1]
    ne = w_router.shape[2]
    assert 2 * nh <= LANES and ne <= LANES

    alpha = (2 * depth) ** 0.25
    off_xbc = ssm_w
    off_dtf = off_xbc + xbc
    off_u = off_dtf + 2 * nh
    off_v = off_u + gw
    off_ga = off_v + gw
    off_gb = off_ga + d
    m_off_u = off_dtf
    m_off_v = m_off_u + gw
    m_off_ga = m_off_v + gw
    m_off_gb = m_off_ga + d

    segs = ((0, ctx_len), (ctx_len, seq))
    tm_moe = 512 if (m_tot * TOP_K) % 512 == 0 else 128

    c_all = jnp.concatenate([c, c_ctx[None, :]], axis=0)
    mods = _ada(c_all, w_ada, b_ada).reshape(depth, bsz + 1, N_MOD, d)

    s_all = jnp.concatenate([ctx, x], axis=1)
    xs = jnp.zeros((m_tot * TOP_K + ne * tm_moe, d // 2), U32)

    def layer_mod(l):
        return jnp.stack(
            [jnp.broadcast_to(mods[l, bsz][None], (bsz, N_MOD, d)), mods[l, :bsz]], axis=1)

    h = _ln_mod(s_all, layer_mod(0), ctx_len, 0, 1)
    for l in range(depth):
        column_major = l % 2 == 1
        mod = layer_mod(l)

        w_main = jnp.concatenate([w_in[l][:, :off_dtf], w_in[l][:, off_u:]], axis=1).astype(BF16)
        b_main = jnp.concatenate([b_in[l][:off_dtf], b_in[l][off_u:]])[None, :]
        w_dt = jnp.pad(w_in[l][:, off_dtf:off_u], ((0, 0), (0, LANES - 2 * nh))).astype(BF16)
        b_dt = jnp.pad(b_in[l][off_dtf:off_u], (0, LANES - 2 * nh))[None, :]

        if column_major:
            h = _to_column_major(h, ctx_len)
        h2 = h.reshape(m_tot, d)
        p_main = _matmul(h2, w_main, b_main, BF16, "in_proj")
        dtraw = _matmul(h2, w_dt, b_dt, F32, "dt_proj").reshape(bsz, s_tot, LANES)
        p3 = p_main.reshape(bsz, s_tot, -1)

        xact = _conv_silu(p3, conv_w[l], conv_b[l], off_xbc, segs)
        alog_row = jnp.pad(jnp.concatenate([a_log_f[l], a_log_b[l]]), (0, LANES - 2 * nh))[None, :]
        dtb_row = jnp.pad(jnp.concatenate([dt_bias_f[l], dt_bias_b[l]]), (0, LANES - 2 * nh))[None, :]
        yf, yb = _ssd(xact, dtraw, alog_row, dtb_row, nh, p, n_state, ctx_len)
        d_exp = jnp.repeat(d_skip[l], p)[None, :]
        g_out = _gmlp(p3, m_off_u, m_off_v, gmlp_ln_g[l], gmlp_ln_b[l], w_s[l], b_s[l])
        ssd_out = _gated_rmsnorm(yf.reshape(m_tot, ssm_w), yb.reshape(m_tot, ssm_w), xact.reshape(m_tot, xbc),
                                 p_main, d_exp, ssm_norm_w[l][None, :])
        merged = _merge(ssd_out, g_out.reshape(m_tot, gw), p_main,
                        m_off_ga, m_off_gb, w_pa[l].astype(BF16), w_pb[l].astype(BF16))
        merged = merged.reshape(bsz, s_tot, d)
        if column_major:
            merged = _to_row_major(merged, ctx_len)

        w_r = jnp.pad(w_router[l], ((0, 0), (0, LANES - ne))).astype(BF16)
        b_r = jnp.pad(b_router[l], (0, LANES - ne), constant_values=MASK_NEG)[None, :]
        s_all, h_moe, idx, wts = _wo_residual(
            merged, w_o[l].astype(BF16), s_all, mod, ln_g[l, 0][None, :], ln_b[l, 0][None, :],
            w_r, b_r, ctx_len, alpha)

        pos, te, nact = _moe_plan(idx.reshape(m_tot, TOP_K), ne, tm_moe)
        xs = _dispatch(h_moe.reshape(m_tot, d // 2), pos, xs)
        ys = _experts(xs, te, nact, w_up[l], b_up[l], w_down[l], b_down[l], tm_moe)
        outs = _combine(ys, pos, wts.reshape(m_tot, TOP_K), s_all.reshape(m_tot, d), mod,
                        layer_mod(l + 1) if l + 1 < depth else None,
                        ln_g[l, 1][None, :], ln_b[l, 1][None, :], s_tot, ctx_len, alpha)
        s_all = outs[0].reshape(bsz, s_tot, d)
        if l + 1 < depth:
            h = outs[1].reshape(bsz, s_tot, d)

    return s_all[:, ctx_len:]
```

```python
import functools

import jax
import jax.numpy as jnp
from jax import lax
from jax.experimental import pallas as pl
from jax.experimental.pallas import tpu as pltpu

F32 = jnp.float32
BF16 = jnp.bfloat16
U32 = jnp.uint32
I32 = jnp.int32

SSM_GROUPS = 8
SSM_CHUNK = 128
TOP_K = 4
GRID_W = 64
N_MOD = 6
SWIGLU_LIMIT = 7.0
SWIGLU_ALPHA = 1.702
LN_EPS = 1e-5

LANES = 128
VMEM_LIMIT = 56 * 1024 * 1024
MASK_NEG = -1e30


def _cparams(sem):
    return pltpu.CompilerParams(dimension_semantics=sem, vmem_limit_bytes=VMEM_LIMIT)


def _tile(n, pref, mult=8):
    if n <= pref:
        return n
    t = (pref // mult) * mult
    while t >= mult:
        if n % t == 0:
            return t
        t -= mult
    raise ValueError(f"no tile for {n} <= {pref}")


def _ln(x):
    mu = jnp.mean(x, axis=-1, keepdims=True)
    xc = x - mu
    var = jnp.mean(xc * xc, axis=-1, keepdims=True)
    return xc * lax.rsqrt(var + LN_EPS)


def _sigmoid(x):
    return 1.0 / (1.0 + jnp.exp(-x))


def _gelu_tanh(x):
    c = 0.7978845608028654
    return 0.5 * x * (1.0 + jnp.tanh(c * (x + 0.044715 * (x * x * x))))


def _softplus(x):
    return jnp.maximum(x, 0.0) + jnp.log(1.0 + jnp.exp(-jnp.abs(x)))


def _pack_bf16_pair(y):
    dh = y.shape[-1] // 2
    lo = lax.bitcast_convert_type(y[:, :dh].astype(BF16).astype(F32), U32)
    hi = lax.bitcast_convert_type(y[:, dh:].astype(BF16).astype(F32), U32)
    return (hi & jnp.uint32(0xFFFF0000)) | (lo >> 16)


def _unpack_bf16_pair(w):
    lo = lax.bitcast_convert_type(w << 16, F32)
    hi = lax.bitcast_convert_type(w & jnp.uint32(0xFFFF0000), F32)
    return lo, hi


def _ada_kernel(c_ref, w_ref, b_ref, o_ref):
    c = c_ref[...]
    a = (c * _sigmoid(c)).astype(BF16)
    o_ref[...] = jnp.dot(a, w_ref[...].astype(BF16), preferred_element_type=F32) + b_ref[...]


def _ada(c_all, w_ada, b_ada):
    depth, d, n = w_ada.shape
    r = c_all.shape[0]
    tn = _tile(n, 1024, LANES)
    return pl.pallas_call(
        _ada_kernel,
        out_shape=jax.ShapeDtypeStruct((depth, r, n), F32),
        grid=(depth, n // tn),
        in_specs=[
            pl.BlockSpec((r, d), lambda l, j: (0, 0)),
            pl.BlockSpec((None, d, tn), lambda l, j: (l, 0, j)),
            pl.BlockSpec((None, 1, tn), lambda l, j: (l, 0, j)),
        ],
        out_specs=pl.BlockSpec((None, r, tn), lambda l, j: (l, 0, j)),
        compiler_params=_cparams(("parallel", "parallel")),
        name="ada_mod",
    )(c_all, w_ada, b_ada.reshape(depth, 1, n))


def _ln_mod_kernel(s_ref, mod_ref, h_ref, *, shift_row, scale_row):
    y = _ln(s_ref[...])
    scale = mod_ref[scale_row:scale_row + 1, :]
    shift = mod_ref[shift_row:shift_row + 1, :]
    h_ref[...] = (y * (1.0 + scale) + shift).astype(h_ref.dtype)


def _ln_mod(s_all, mod, ctx_len, shift_row, scale_row):
    b, s, d = s_all.shape
    tr = _tile(ctx_len, 256)
    nct = ctx_len // tr
    return pl.pallas_call(
        functools.partial(_ln_mod_kernel, shift_row=shift_row, scale_row=scale_row),
        out_shape=jax.ShapeDtypeStruct((b, s, d), BF16),
        grid=(b, s // tr),
        in_specs=[
            pl.BlockSpec((None, tr, d), lambda bi, i: (bi, i, 0)),
            pl.BlockSpec((None, None, N_MOD, d), lambda bi, i: (bi, jnp.minimum(i // nct, 1), 0, 0)),
        ],
        out_specs=pl.BlockSpec((None, tr, d), lambda bi, i: (bi, i, 0)),
        compiler_params=_cparams(("parallel", "parallel")),
        name="ln_mod",
    )(s_all, mod)


def _mm_kernel(a_ref, w_ref, b_ref, o_ref):
    acc = jnp.dot(a_ref[...], w_ref[...], preferred_element_type=F32)
    o_ref[...] = (acc + b_ref[...]).astype(o_ref.dtype)


def _matmul(a, w, layer, bias, out_dtype, name):
    m, k = a.shape
    n = w.shape[2]
    tm = _tile(m, 1024)
    tn = _tile(n, 1024, LANES)
    return pl.pallas_call(
        _mm_kernel,
        out_shape=jax.ShapeDtypeStruct((m, n), out_dtype),
        grid=(n // tn, m // tm),
        in_specs=[
            pl.BlockSpec((tm, k), lambda j, i: (i, 0)),
            pl.BlockSpec((None, k, tn), lambda j, i: (layer, 0, j)),
            pl.BlockSpec((1, tn), lambda j, i: (0, j)),
        ],
        out_specs=pl.BlockSpec((tm, tn), lambda j, i: (i, j)),
        compiler_params=_cparams(("parallel", "parallel")),
        name=name,
    )(a, w, bias)


_CONV_ROWS = 128
_CONV_WIN = 256


def _conv_kernel(x_ref, w_ref, b_ref, o_ref, *, segs, kw):
    half = kw // 2
    rb = _CONV_ROWS
    shift_mats = {}

    taps = [k for k in range(kw) if k != half]

    def shifts(win, off):
        if (win, off) not in shift_mats:
            t = lax.broadcasted_iota(I32, (rb, win), 0)
            j = lax.broadcasted_iota(I32, (rb, win), 1)
            shift_mats[(win, off)] = jnp.concatenate(
                [jnp.where(j == t + (off + k - half), 1.0, 0.0).astype(BF16) for k in taps], axis=0)
        return shift_mats[(win, off)]

    for st, ln in segs:
        win = min(_CONV_WIN, ln)
        for r0 in range(st, st + ln, rb):
            lo = min(max(r0 - (win - rb) // 2, st), st + ln - win)
            shifted = jnp.dot(shifts(win, r0 - lo), x_ref[lo:lo + win, :], preferred_element_type=F32)
            acc = b_ref[...] + w_ref[half:half + 1, :] * x_ref[r0:r0 + rb, :].astype(F32)
            for n, k in enumerate(taps):
                acc = acc + w_ref[k:k + 1, :] * shifted[n * rb:(n + 1) * rb, :]
            o_ref[r0:r0 + rb, :] = (acc * _sigmoid(acc)).astype(o_ref.dtype)


def _conv_silu(p_main, conv_w, conv_b, col_off, segs):
    b, s, _ = p_main.shape
    kw, xbc = conv_w.shape
    cw = 512 if (xbc % 512 == 0 and col_off % 512 == 0) else LANES
    assert xbc % cw == 0 and col_off % cw == 0
    assert all(ln % _CONV_ROWS == 0 and (ln >= _CONV_WIN or ln == _CONV_ROWS) for _, ln in segs)
    assert kw // 2 <= (_CONV_WIN - _CONV_ROWS) // 2
    ob = col_off // cw
    return pl.pallas_call(
        functools.partial(_conv_kernel, segs=segs, kw=kw),
        out_shape=jax.ShapeDtypeStruct((b, s, xbc), BF16),
        grid=(b, xbc // cw),
        in_specs=[
            pl.BlockSpec((None, s, cw), lambda bi, j: (bi, 0, ob + j)),
            pl.BlockSpec((kw, cw), lambda bi, j: (0, j)),
            pl.BlockSpec((1, cw), lambda bi, j: (0, j)),
        ],
        out_specs=pl.BlockSpec((None, s, cw), lambda bi, j: (bi, 0, j)),
        compiler_params=_cparams(("parallel", "parallel")),
        name="conv_silu",
    )(p_main, conv_w, conv_b.reshape(1, xbc))


def _head_rows(v_t, c0, hpg, p):
    q = v_t.shape[1]
    return jnp.concatenate(
        [jnp.broadcast_to(v_t[c0 + h:c0 + h + 1, :], (p, q)) for h in range(hpg)], axis=0)


def _ssd_dir(xa_ref, dtr_ref, arow, dtb_row, y_ref, st_ref, *, backward, nh, ng, p, n, q):
    hpg = nh // ng
    width = nh * p
    gw = hpg * p
    col0 = nh if backward else 0
    dt = _softplus(dtr_ref[...] + dtb_row)
    da = dt * arow
    ri = lax.broadcasted_iota(I32, (q, q), 0)
    ci = lax.broadcasted_iota(I32, (q, q), 1)
    tri = jnp.where((ri <= ci) if backward else (ri >= ci), 1.0, 0.0).astype(F32)
    cs = jnp.dot(tri, da, preferred_element_type=F32, precision=lax.Precision.HIGHEST)
    tot = cs[0:1, :] if backward else cs[q - 1:q, :]
    cdec = jnp.exp(tot)
    dt_t = dt.T
    cs_t = cs.T
    wts_t = (dt * jnp.exp(tot - cs)).T
    ecs_t = jnp.exp(cs).T
    mask_t = (ci <= ri) if backward else (ci >= ri)
    eye_q = jnp.where(ri == ci, 1.0, 0.0).astype(BF16)
    eye_w = jnp.where(lax.broadcasted_iota(I32, (gw, gw), 0) == lax.broadcasted_iota(I32, (gw, gw), 1),
                      1.0, 0.0).astype(BF16)
    nt_dims = (((1,), (1,)), ((), ()))
    for g in range(ng):
        x_t = xa_ref[:, g * gw:(g + 1) * gw].T.astype(F32)
        bg = xa_ref[:, width + g * n:width + (g + 1) * n]
        cg = xa_ref[:, width + ng * n + g * n:width + ng * n + (g + 1) * n]
        cb_t = lax.dot_general(bg, cg, nt_dims, preferred_element_type=F32)
        c0 = col0 + g * hpg
        yd = []
        for h in range(hpg):
            c = c0 + h
            seg = cs_t[c:c + 1, :] - cs[:, c:c + 1]
            m_t = (cb_t * jnp.exp(jnp.where(mask_t, seg, MASK_NEG))).astype(BF16)
            xdt_t = (x_t[h * p:(h + 1) * p, :] * dt_t[c:c + 1, :]).astype(BF16)
            yd.append(jnp.dot(xdt_t, m_t, preferred_element_type=F32))
        st_old = st_ref[g]
        yo_t = lax.dot_general(st_old.astype(BF16), cg, nt_dims, preferred_element_type=F32)
        y_t = jnp.concatenate(yd, axis=0) + yo_t * _head_rows(ecs_t, c0, hpg, p)
        xw_t = (x_t * _head_rows(wts_t, c0, hpg, p)).astype(BF16)
        cdec_rows = jnp.concatenate(
            [jnp.broadcast_to(cdec[:, c0 + h:c0 + h + 1], (p, n)) for h in range(hpg)], axis=0)
        st_ref[g] = st_old * cdec_rows + jnp.dot(xw_t, bg, preferred_element_type=F32)
        y_ref[:, g * gw:(g + 1) * gw] = y_t.T.astype(y_ref.dtype)


def _ssd_kernel(xf_ref, xb_ref, dtf_ref, dtb_ref, alog_ref, dtbias_ref, yf_ref, yb_ref,
                sf_ref, sb_ref, **kw):
    @pl.when(pl.program_id(1) == 0)
    def _():
        sf_ref[...] = jnp.zeros_like(sf_ref)
        sb_ref[...] = jnp.zeros_like(sb_ref)

    arow = -jnp.exp(alog_ref[...])
    dtb_row = dtbias_ref[...]
    _ssd_dir(xf_ref, dtf_ref, arow, dtb_row, yf_ref, sf_ref, backward=False, **kw)
    _ssd_dir(xb_ref, dtb_ref, arow, dtb_row, yb_ref, sb_ref, backward=True, **kw)


def _ssd(xact, dtraw, alog_row, dtbias_row, nh, p, n, ctx_len):
    b, s, xbc = xact.shape
    q = SSM_CHUNK
    ng = SSM_GROUPS
    width = nh * p
    ncc = ctx_len // q
    nc = s // q
    ncl = nc - ncc

    def fwd(bi, i):
        return (bi, i, 0)

    def bwd(bi, i):
        return (bi, jnp.where(i < ncc, ncc - 1 - i, 2 * ncc + ncl - 1 - i), 0)

    kern = functools.partial(_ssd_kernel, nh=nh, ng=ng, p=p, n=n, q=q)
    return pl.pallas_call(
        kern,
        out_shape=(jax.ShapeDtypeStruct((b, s, width), BF16), jax.ShapeDtypeStruct((b, s, width), BF16)),
        grid=(b, nc),
        in_specs=[
            pl.BlockSpec((None, q, xbc), fwd),
            pl.BlockSpec((None, q, xbc), bwd),
            pl.BlockSpec((None, q, LANES), fwd),
            pl.BlockSpec((None, q, LANES), bwd),
            pl.BlockSpec((1, LANES), lambda bi, i: (0, 0)),
            pl.BlockSpec((1, LANES), lambda bi, i: (0, 0)),
        ],
        out_specs=(pl.BlockSpec((None, q, width), fwd), pl.BlockSpec((None, q, width), bwd)),
        scratch_shapes=[pltpu.VMEM((ng, (nh // ng) * p, n), F32), pltpu.VMEM((ng, (nh // ng) * p, n), F32)],
        compiler_params=_cparams(("parallel", "arbitrary")),
        name="ssd_scan",
    )(xact, xact, dtraw, dtraw, alog_row, dtbias_row)


def _gnorm_kernel(yf_ref, yb_ref, x_ref, z_ref, d_ref, w_ref, o_ref, *, ng):
    x = x_ref[...].astype(F32)
    z = z_ref[...].astype(F32)
    y = yf_ref[...].astype(F32) + yb_ref[...].astype(F32) + d_ref[...] * x
    g = y * (z * _sigmoid(z))
    gw = g.shape[-1] // ng
    for k in range(ng):
        gk = g[:, k * gw:(k + 1) * gw]
        ms = jnp.mean(gk * gk, axis=-1, keepdims=True)
        o_ref[:, k * gw:(k + 1) * gw] = (
            gk * lax.rsqrt(ms + LN_EPS) * w_ref[:, k * gw:(k + 1) * gw]).astype(o_ref.dtype)


def _gated_rmsnorm(yf, yb, xact2d, p_main2d, d_exp, norm_w):
    m, width = yf.shape
    tr = _tile(m, 256)
    blk = pl.BlockSpec((tr, width), lambda i: (i, 0))
    row = pl.BlockSpec((1, width), lambda i: (0, 0))
    return pl.pallas_call(
        functools.partial(_gnorm_kernel, ng=SSM_GROUPS),
        out_shape=jax.ShapeDtypeStruct((m, width), BF16),
        grid=(m // tr,),
        in_specs=[blk, blk, blk, blk, row, row],
        out_specs=blk,
        compiler_params=_cparams(("parallel",)),
        name="gated_rmsnorm",
    )(yf, yb, xact2d, p_main2d, d_exp, norm_w)


def _gmlp_kernel(u_ref, v_ref, lng_ref, lnb_ref, ws_ref, bst_ref, o_ref, *, ng):
    v = _gelu_tanh(v_ref[...].astype(F32))
    vb = (_ln(v) * lng_ref[...] + lnb_ref[...]).astype(BF16)
    u = _gelu_tanh(u_ref[...].astype(F32))
    gw = u.shape[-1] // ng
    for g in range(ng):
        mixed = jnp.dot(ws_ref[g], vb[:, g * gw:(g + 1) * gw], preferred_element_type=F32)
        mixed = mixed + bst_ref[:, g:g + 1]
        o_ref[:, g * gw:(g + 1) * gw] = (u[:, g * gw:(g + 1) * gw] * mixed).astype(o_ref.dtype)


def _gmlp(p_main, off_u, off_v, ln_g, ln_b, w_s, layer, b_s):
    b, s, _ = p_main.shape
    _, ng, q, _ = w_s.shape
    wg = ln_g.shape[-1]
    assert off_u % wg == 0 and off_v % wg == 0
    bu, bv = off_u // wg, off_v // wg
    row = pl.BlockSpec((1, wg), lambda bi, c: (0, 0))
    return pl.pallas_call(
        functools.partial(_gmlp_kernel, ng=ng),
        out_shape=jax.ShapeDtypeStruct((b, s, wg), BF16),
        grid=(b, s // q),
        in_specs=[
            pl.BlockSpec((None, q, wg), lambda bi, c: (bi, c, bu)),
            pl.BlockSpec((None, q, wg), lambda bi, c: (bi, c, bv)),
            row, row,
            pl.BlockSpec((None, ng, q, q), lambda bi, c: (layer, 0, 0, 0)),
            pl.BlockSpec((q, ng), lambda bi, c: (0, 0)),
        ],
        out_specs=pl.BlockSpec((None, q, wg), lambda bi, c: (bi, c, 0)),
        compiler_params=_cparams(("parallel", "parallel")),
        name="gmlp",
    )(p_main, p_main, ln_g.reshape(1, wg), ln_b.reshape(1, wg), w_s, b_s.T)


def _merge_kernel(a_ref, g_ref, ga_ref, gb_ref, wpa_ref, wpb_ref, o_ref):
    pa = jnp.dot(a_ref[...], wpa_ref[...], preferred_element_type=F32)
    pb = jnp.dot(g_ref[...], wpb_ref[...], preferred_element_type=F32)
    m = _sigmoid(ga_ref[...].astype(F32)) * pa + _sigmoid(gb_ref[...].astype(F32)) * pb
    o_ref[...] = m.astype(o_ref.dtype)


def _merge(ssd_out, g_out, p_main2d, off_ga, off_gb, w_pa, w_pb, layer):
    m, wa = ssd_out.shape
    wb = g_out.shape[1]
    n = w_pa.shape[2]
    tm = _tile(m, 512)
    tn = _tile(n, 1024, LANES)
    assert off_ga % tn == 0 and off_gb % tn == 0
    ba, bb = off_ga // tn, off_gb // tn
    return pl.pallas_call(
        _merge_kernel,
        out_shape=jax.ShapeDtypeStruct((m, n), BF16),
        grid=(n // tn, m // tm),
        in_specs=[
            pl.BlockSpec((tm, wa), lambda j, i: (i, 0)),
            pl.BlockSpec((tm, wb), lambda j, i: (i, 0)),
            pl.BlockSpec((tm, tn), lambda j, i: (i, ba + j)),
            pl.BlockSpec((tm, tn), lambda j, i: (i, bb + j)),
            pl.BlockSpec((None, wa, tn), lambda j, i: (layer, 0, j)),
            pl.BlockSpec((None, wb, tn), lambda j, i: (layer, 0, j)),
        ],
        out_specs=pl.BlockSpec((tm, tn), lambda j, i: (i, j)),
        compiler_params=_cparams(("parallel", "parallel")),
        name="merge_branches",
    )(ssd_out, g_out, p_main2d, p_main2d, w_pa, w_pb)


def _top_k(logits, k):
    lane = lax.broadcasted_iota(I32, logits.shape, 1)
    vals, idxs = [], []
    cur = logits
    for _ in range(k):
        m = jnp.max(cur, axis=-1, keepdims=True)
        idx = jnp.min(jnp.where(cur == m, lane, LANES), axis=-1, keepdims=True)
        vals.append(m)
        idxs.append(idx)
        cur = jnp.where(lane == idx, -jnp.inf, cur)
    return vals, idxs


def _wo_res_kernel(m_ref, wo_ref, s_ref, modc_ref, modl_ref, lng_ref, lnb_ref, wr_ref, br_ref,
                   snew_ref, h_ref, idx_ref, wts_ref, *, alpha, k, sub, nctx):
    y = jnp.dot(m_ref[...], wo_ref[...], preferred_element_type=F32)
    first = pl.program_id(1) == 0
    for r in range(m_ref.shape[0] // sub):
        rows = slice(r * sub, (r + 1) * sub)
        mod = jnp.where(first, modc_ref[...], modl_ref[...]) if r < nctx else modl_ref[...]
        t = alpha * s_ref[rows, :] + mod[2:3, :] * y[rows, :]
        sn = _ln(t) * lng_ref[...] + lnb_ref[...]
        snew_ref[rows, :] = sn
        h = _ln(sn) * (1.0 + mod[4:5, :]) + mod[3:4, :]
        h_ref[rows, :] = _pack_bf16_pair(h)
        logits = jnp.dot(h.astype(BF16), wr_ref[...], preferred_element_type=F32) + br_ref[...]
        vals, idxs = _top_k(logits, k)
        es = [jnp.exp(v - vals[0]) for v in vals]
        den = es[0]
        for e in es[1:]:
            den = den + e
        for j in range(k):
            idx_ref[rows, j:j + 1] = idxs[j]
            wts_ref[rows, j:j + 1] = es[j] / den


_WO_ROWS = 768


def _wo_residual(m_all, w_o, w_router, layer, s_all, mod, ln_g, ln_b, b_router, ctx_len, alpha):
    b, s, d = s_all.shape
    sub = _tile(ctx_len, 256)
    nctx = ctx_len // sub
    tr = sub * max(t for t in range(1, s // sub + 1)
                   if (s // sub) % t == 0 and t >= nctx and t * sub <= max(_WO_ROWS, ctx_len))
    blk = lambda w: pl.BlockSpec((None, tr, w), lambda bi, i: (bi, i, 0))
    row = pl.BlockSpec((1, d), lambda bi, i: (0, 0))
    once = dict(pipeline_mode=pl.Buffered(1))
    return pl.pallas_call(
        functools.partial(_wo_res_kernel, alpha=alpha, k=TOP_K, sub=sub, nctx=nctx),
        out_shape=(
            jax.ShapeDtypeStruct((b, s, d), F32),
            jax.ShapeDtypeStruct((b, s, d // 2), U32),
            jax.ShapeDtypeStruct((b, s, TOP_K), I32),
            jax.ShapeDtypeStruct((b, s, TOP_K), F32),
        ),
        grid=(b, s // tr),
        in_specs=[
            blk(d),
            pl.BlockSpec((None, d, d), lambda bi, i: (layer, 0, 0), **once),
            blk(d),
            pl.BlockSpec((None, None, N_MOD, d), lambda bi, i: (bi, 0, 0, 0)),
            pl.BlockSpec((None, None, N_MOD, d), lambda bi, i: (bi, 1, 0, 0)),
            row, row,
            pl.BlockSpec((None, d, LANES), lambda bi, i: (layer, 0, 0), **once),
            pl.BlockSpec((1, LANES), lambda bi, i: (0, 0)),
        ],
        out_specs=(blk(d), blk(d // 2), blk(TOP_K), blk(TOP_K)),
        compiler_params=_cparams(("parallel", "arbitrary")),
        name="wo_residual_router",
    )(m_all, w_o, s_all, mod, mod, ln_g, ln_b, w_router, b_router)


def _moe_plan(idx, ne, tm):
    m, k = idx.shape
    na = m * k
    assert na % tm == 0
    e = idx.reshape(na)
    onehot = (e[:, None] == jnp.arange(ne, dtype=I32)[None, :]).astype(I32)
    csum = jnp.cumsum(onehot, axis=0)
    rank = jnp.sum(csum * onehot, axis=1) - 1
    counts = csum[-1]
    ptiles = (counts + tm - 1) // tm
    tile_end = jnp.cumsum(ptiles)
    tile_start = tile_end - ptiles
    pos = (jnp.sum(onehot * tile_start[None, :], axis=1) * tm + rank).astype(I32)
    nt = na // tm + ne
    tids = jnp.arange(nt, dtype=I32)
    te = jnp.minimum(jnp.sum((tids[:, None] >= tile_end[None, :]).astype(I32), axis=1), ne - 1)
    return pos, te.astype(I32), tile_end[-1:].astype(I32)


_ISSUE_UNROLL = 8


def _dispatch_kernel(pos_ref, h_ref, xs_in, xs_out, sem, *, k):
    del xs_in
    tr = h_ref.shape[0]

    def issue(i, c):
        for u in range(_ISSUE_UNROLL):
            r = i * _ISSUE_UNROLL + u
            for j in range(k):
                pltpu.make_async_copy(
                    h_ref.at[pl.ds(r, 1)], xs_out.at[pl.ds(pos_ref[0, r * k + j], 1)], sem
                ).start(priority=j % 2)
        return c

    lax.fori_loop(0, tr // _ISSUE_UNROLL, issue, 0)
    for j in range(k):
        pltpu.make_async_copy(h_ref, xs_out.at[pl.ds(0, tr)], sem).wait()


def _dispatch(h_packed, pos, xs):
    m, dh = h_packed.shape
    k = pos.shape[0] // m
    tr = _tile(m, 256, _ISSUE_UNROLL)
    return pl.pallas_call(
        functools.partial(_dispatch_kernel, k=k),
        out_shape=jax.ShapeDtypeStruct(xs.shape, xs.dtype),
        grid=(m // tr,),
        in_specs=[
            pl.BlockSpec((None, 1, tr * k), lambda i: (i, 0, 0), memory_space=pltpu.SMEM),
            pl.BlockSpec((tr, dh), lambda i: (i, 0)),
            pl.BlockSpec(memory_space=pl.ANY),
        ],
        out_specs=pl.BlockSpec(memory_space=pl.ANY),
        scratch_shapes=[pltpu.SemaphoreType.DMA(())],
        input_output_aliases={2: 0},
        compiler_params=_cparams(("arbitrary",)),
        name="moe_dispatch",
    )(pos.reshape(m // tr, 1, tr * k), h_packed, xs)


def _expert_kernel(te_ref, nact_ref, x_ref, wup_ref, bup_ref, wdn_ref, bdn_ref, y_ref, wup_bf, wdn_bf):
    t = pl.program_id(0)
    active = t < nact_ref[0]

    @pl.when(jnp.logical_not(active))
    def _():
        y_ref[...] = jnp.zeros_like(y_ref)

    @pl.when(jnp.logical_and(active, jnp.logical_or(t == 0, te_ref[t] != te_ref[jnp.maximum(t - 1, 0)])))
    def _():
        wup_bf[...] = wup_ref[...].astype(BF16)
        wdn_bf[...] = wdn_ref[...].astype(BF16)

    @pl.when(active)
    def _():
        dh = x_ref.shape[-1]
        lo, hi = _unpack_bf16_pair(x_ref[...])
        hid = jnp.dot(lo.astype(BF16), wup_bf[0:dh, :], preferred_element_type=F32)
        hid = hid + jnp.dot(hi.astype(BF16), wup_bf[dh:, :], preferred_element_type=F32)
        hid = hid + bup_ref[...]
        f = hid.shape[-1] // 2
        glu = jnp.minimum(hid[:, :f], SWIGLU_LIMIT)
        lin = jnp.clip(hid[:, f:], -SWIGLU_LIMIT, SWIGLU_LIMIT)
        act = (glu * _sigmoid(SWIGLU_ALPHA * glu) * (lin + 1.0)).astype(BF16)
        y = jnp.dot(act, wdn_bf[...], preferred_element_type=F32) + bdn_ref[...]
        y_ref[...] = _pack_bf16_pair(y)


def _experts(xs, te, nact, w_up, b_up, w_down, b_down, layer, tm):
    p_rows, dh = xs.shape
    _, ne, d, f2 = w_up.shape
    f = f2 // 2
    nt = p_rows // tm

    def row_map(t, te, nact):
        return (jnp.minimum(t, nact[0] - 1), 0)

    def w_map(t, te, nact):
        return (layer, te[t], 0, 0)

    def b_map(t, te, nact):
        return (te[t], 0, 0)

    grid_spec = pltpu.PrefetchScalarGridSpec(
        num_scalar_prefetch=2,
        grid=(nt,),
        in_specs=[
            pl.BlockSpec((tm, dh), row_map),
            pl.BlockSpec((None, None, d, f2), w_map),
            pl.BlockSpec((None, 1, f2), b_map),
            pl.BlockSpec((None, None, f, d), w_map),
            pl.BlockSpec((None, 1, d), b_map),
        ],
        out_specs=pl.BlockSpec((tm, dh), lambda t, te, nact: (t, 0)),
        scratch_shapes=[pltpu.VMEM((d, f2), BF16), pltpu.VMEM((f, d), BF16)],
    )
    return pl.pallas_call(
        _expert_kernel,
        out_shape=jax.ShapeDtypeStruct((p_rows, dh), U32),
        grid_spec=grid_spec,
        compiler_params=_cparams(("arbitrary",)),
        name="moe_experts",
    )(te, nact, xs, w_up, b_up.reshape(ne, 1, f2), w_down, b_down.reshape(ne, 1, d))


def _combine_kernel(pos_ref, posn_ref, wts_ref, s_ref, mod_ref, modn_ref, lng_ref, lnb_ref, ys_hbm,
                    snew_ref, *rest, alpha, k, with_next):
    if with_next:
        hn_ref, ybuf, sems = rest
    else:
        ybuf, sems = rest
    i = pl.program_id(0)
    n = pl.num_programs(0)
    tr = s_ref.shape[0]
    slot = lax.rem(i, 2)

    def issue(p_ref, sl):
        def body(it, c):
            for u in range(_ISSUE_UNROLL):
                r = it * _ISSUE_UNROLL + u
                for j in range(k):
                    pltpu.make_async_copy(
                        ys_hbm.at[pl.ds(p_ref[0, r * k + j], 1)], ybuf.at[sl, j, pl.ds(r, 1)], sems.at[sl]
                    ).start(priority=j % 2)
            return c

        lax.fori_loop(0, tr // _ISSUE_UNROLL, body, 0)

    @pl.when(i == 0)
    def _():
        issue(pos_ref, 0)

    for sl in range(2):
        @pl.when(jnp.logical_and(i + 1 < n, slot == 1 - sl))
        def _():
            issue(posn_ref, sl)

    for j in range(k):
        pltpu.make_async_copy(ys_hbm.at[pl.ds(0, tr)], ybuf.at[slot, j], sems.at[slot]).wait()

    acc = None
    for j in range(k):
        lo, hi = _unpack_bf16_pair(ybuf[slot, j])
        yk = wts_ref[:, j:j + 1] * jnp.concatenate([lo, hi], axis=1)
        acc = yk if acc is None else acc + yk
    t = alpha * s_ref[...] + mod_ref[5:6, :] * acc
    sn = _ln(t) * lng_ref[...] + lnb_ref[...]
    snew_ref[...] = sn
    if with_next:
        hn_ref[...] = (_ln(sn) * (1.0 + modn_ref[1:2, :]) + modn_ref[0:1, :]).astype(hn_ref.dtype)


def _combine(ys, pos, wts, s2, mod, mod_next, ln_g, ln_b, seg_rows, ctx_len, alpha):
    m, d = s2.shape
    k = wts.shape[1]
    dh = ys.shape[1]
    tr = _tile(ctx_len, 256, _ISSUE_UNROLL)
    nt = m // tr
    spt = seg_rows // tr
    nct = ctx_len // tr
    pos3 = pos.reshape(nt, 1, tr * k)
    with_next = mod_next is not None
    row = pl.BlockSpec((1, d), lambda i: (0, 0))
    rows = pl.BlockSpec((tr, d), lambda i: (i, 0))
    modspec = pl.BlockSpec((None, None, N_MOD, d),
                           lambda i: (i // spt, jnp.minimum((i % spt) // nct, 1), 0, 0))
    out_shape = [jax.ShapeDtypeStruct((m, d), F32)]
    out_specs = [rows]
    if with_next:
        out_shape.append(jax.ShapeDtypeStruct((m, d), BF16))
        out_specs.append(rows)
    return pl.pallas_call(
        functools.partial(_combine_kernel, alpha=alpha, k=k, with_next=with_next),
        out_shape=tuple(out_shape),
        grid=(nt,),
        in_specs=[
            pl.BlockSpec((None, 1, tr * k), lambda i: (i, 0, 0), memory_space=pltpu.SMEM),
            pl.BlockSpec((None, 1, tr * k), lambda i: (jnp.minimum(i + 1, nt - 1), 0, 0),
                         memory_space=pltpu.SMEM),
            pl.BlockSpec((tr, k), lambda i: (i, 0)),
            rows, modspec, modspec, row, row,
            pl.BlockSpec(memory_space=pl.ANY),
        ],
        out_specs=tuple(out_specs),
        scratch_shapes=[pltpu.VMEM((2, k, tr, dh), U32), pltpu.SemaphoreType.DMA((2,))],
        compiler_params=_cparams(("arbitrary",)),
        name="moe_combine_residual",
    )(pos3, pos3, wts, s2, mod, mod_next if with_next else mod, ln_g, ln_b, ys)


def _to_column_major(t, ctx_len):
    b, s, d = t.shape
    lat = t[:, ctx_len:]
    rows = (s - ctx_len) // GRID_W
    lat = lat.reshape(b, rows, GRID_W, d).swapaxes(1, 2).reshape(b, s - ctx_len, d)
    return jnp.concatenate([t[:, :ctx_len], lat], axis=1)


def _to_row_major(t, ctx_len):
    b, s, d = t.shape
    lat = t[:, ctx_len:]
    rows = (s - ctx_len) // GRID_W
    lat = lat.reshape(b, GRID_W, rows, d).swapaxes(1, 2).reshape(b, s - ctx_len, d)
    return jnp.concatenate([t[:, :ctx_len], lat], axis=1)


def kernel(x, c, ctx, c_ctx, w_ada, b_ada, w_in, b_in, conv_w, conv_b, a_log_f, a_log_b, dt_bias_f, dt_bias_b, d_skip, ssm_norm_w, gmlp_ln_g, gmlp_ln_b, w_s, b_s, w_pa, w_pb, w_o, ln_g, ln_b, w_router, b_router, w_up, b_up, w_down, b_down):
    bsz, seq, d = x.shape
    ctx_len = ctx.shape[1]
    depth = w_ada.shape[0]
    s_tot = ctx_len + seq
    m_tot = bsz * s_tot
    nh = a_log_f.shape[1]
    ssm_w = ssm_norm_w.shape[1]
    p = ssm_w // nh
    xbc = conv_w.shape[2]
    n_state = (xbc - ssm_w) // (2 * SSM_GROUPS)
    gw = gmlp_ln_g.shape[1]
    ne = w_router.shape[2]
    assert 2 * nh <= LANES and ne <= LANES

    alpha = (2 * depth) ** 0.25
    off_xbc = ssm_w
    off_dtf = off_xbc + xbc
    off_u = off_dtf + 2 * nh
    off_v = off_u + gw
    off_ga = off_v + gw
    off_gb = off_ga + d
    m_off_u = off_dtf
    m_off_v = m_off_u + gw
    m_off_ga = m_off_v + gw
    m_off_gb = m_off_ga + d

    segs = ((0, ctx_len), (ctx_len, seq))
    tm_moe = 512 if (m_tot * TOP_K) % 512 == 0 else 128

    c_all = jnp.concatenate([c, c_ctx[None, :]], axis=0)
    mods = _ada(c_all, w_ada, b_ada).reshape(depth, bsz + 1, N_MOD, d)

    s_all = jnp.concatenate([ctx, x], axis=1)
    xs = jnp.zeros((m_tot * TOP_K + ne * tm_moe, d // 2), U32)

    def layer_mod(l):
        return jnp.stack(
            [jnp.broadcast_to(mods[l, bsz][None], (bsz, N_MOD, d)), mods[l, :bsz]], axis=1)

    w_main = jnp.concatenate([w_in[:, :, :off_dtf], w_in[:, :, off_u:]], axis=2).astype(BF16)
    w_dt = jnp.pad(w_in[:, :, off_dtf:off_u], ((0, 0), (0, 0), (0, LANES - 2 * nh))).astype(BF16)
    w_pa_b, w_pb_b, w_o_b, w_s_b = (w.astype(BF16) for w in (w_pa, w_pb, w_o, w_s))
    w_r_b = jnp.pad(w_router, ((0, 0), (0, 0), (0, LANES - ne))).astype(BF16)

    h = _ln_mod(s_all, layer_mod(0), ctx_len, 0, 1)
    for l in range(depth):
        column_major = l % 2 == 1
        mod = layer_mod(l)

        b_main = jnp.concatenate([b_in[l][:off_dtf], b_in[l][off_u:]])[None, :]
        b_dt = jnp.pad(b_in[l][off_dtf:off_u], (0, LANES - 2 * nh))[None, :]

        if column_major:
            h = _to_column_major(h, ctx_len)
        h2 = h.reshape(m_tot, d)
        p_main = _matmul(h2, w_main, l, b_main, BF16, "in_proj")
        dtraw = _matmul(h2, w_dt, l, b_dt, F32, "dt_proj").reshape(bsz, s_tot, LANES)
        p3 = p_main.reshape(bsz, s_tot, -1)

        xact = _conv_silu(p3, conv_w[l], conv_b[l], off_xbc, segs)
        alog_row = jnp.pad(jnp.concatenate([a_log_f[l], a_log_b[l]]), (0, LANES - 2 * nh))[None, :]
        dtb_row = jnp.pad(jnp.concatenate([dt_bias_f[l], dt_bias_b[l]]), (0, LANES - 2 * nh))[None, :]
        yf, yb = _ssd(xact, dtraw, alog_row, dtb_row, nh, p, n_state, ctx_len)
        d_exp = jnp.repeat(d_skip[l], p)[None, :]
        g_out = _gmlp(p3, m_off_u, m_off_v, gmlp_ln_g[l], gmlp_ln_b[l], w_s_b, l, b_s[l])
        ssd_out = _gated_rmsnorm(yf.reshape(m_tot, ssm_w), yb.reshape(m_tot, ssm_w), xact.reshape(m_tot, xbc),
                                 p_main, d_exp, ssm_norm_w[l][None, :])
        merged = _merge(ssd_out, g_out.reshape(m_tot, gw), p_main, m_off_ga, m_off_gb, w_pa_b, w_pb_b, l)
        merged = merged.reshape(bsz, s_tot, d)
        if column_major:
            merged = _to_row_major(merged, ctx_len)

        b_r = jnp.pad(b_router[l], (0, LANES - ne), constant_values=MASK_NEG)[None, :]
        s_all, h_moe, idx, wts = _wo_residual(
            merged, w_o_b, w_r_b, l, s_all, mod, ln_g[l, 0][None, :], ln_b[l, 0][None, :],
            b_r, ctx_len, alpha)

        pos, te, nact = _moe_plan(idx.reshape(m_tot, TOP_K), ne, tm_moe)
        xs = _dispatch(h_moe.reshape(m_tot, d // 2), pos, xs)
        ys = _experts(xs, te, nact, w_up, b_up[l], w_down, b_down[l], l, tm_moe)
        outs = _combine(ys, pos, wts.reshape(m_tot, TOP_K), s_all.reshape(m_tot, d), mod,
                        layer_mod(l + 1) if l + 1 < depth else None,
                        ln_g[l, 1][None, :], ln_b[l, 1][None, :], s_tot, ctx_len, alpha)
        s_all = outs[0].reshape(bsz, s_tot, d)
        if l + 1 < depth:
            h = outs[1].reshape(bsz, s_tot, d)

    return s_all[:, ctx_len:]
```

```python
import functools

import jax
import jax.numpy as jnp
from jax import lax
from jax.experimental import pallas as pl
from jax.experimental.pallas import tpu as pltpu

F32 = jnp.float32
BF16 = jnp.bfloat16
U32 = jnp.uint32
I32 = jnp.int32

SSM_GROUPS = 8
SSM_CHUNK = 128
TOP_K = 4
GRID_W = 64
N_MOD = 6
SWIGLU_LIMIT = 7.0
SWIGLU_ALPHA = 1.702
LN_EPS = 1e-5

LANES = 128
VMEM_LIMIT = 56 * 1024 * 1024
MASK_NEG = -1e30


def _cparams(sem):
    return pltpu.CompilerParams(dimension_semantics=sem, vmem_limit_bytes=VMEM_LIMIT)


def _tile(n, pref, mult=8):
    if n <= pref:
        return n
    t = (pref // mult) * mult
    while t >= mult:
        if n % t == 0:
            return t
        t -= mult
    raise ValueError(f"no tile for {n} <= {pref}")


def _ln(x):
    mu = jnp.mean(x, axis=-1, keepdims=True)
    xc = x - mu
    var = jnp.mean(xc * xc, axis=-1, keepdims=True)
    return xc * lax.rsqrt(var + LN_EPS)


def _sigmoid(x):
    return 1.0 / (1.0 + jnp.exp(-x))


def _gelu_tanh(x):
    c = 0.7978845608028654
    return 0.5 * x * (1.0 + jnp.tanh(c * (x + 0.044715 * (x * x * x))))


def _softplus(x):
    return jnp.maximum(x, 0.0) + jnp.log(1.0 + jnp.exp(-jnp.abs(x)))


def _pack_bf16_pair(y):
    dh = y.shape[-1] // 2
    lo = lax.bitcast_convert_type(y[:, :dh].astype(BF16).astype(F32), U32)
    hi = lax.bitcast_convert_type(y[:, dh:].astype(BF16).astype(F32), U32)
    return (hi & jnp.uint32(0xFFFF0000)) | (lo >> 16)


def _unpack_bf16_pair(w):
    lo = lax.bitcast_convert_type(w << 16, F32)
    hi = lax.bitcast_convert_type(w & jnp.uint32(0xFFFF0000), F32)
    return lo, hi


def _ada_kernel(c_ref, w_ref, b_ref, o_ref):
    c = c_ref[...]
    a = (c * _sigmoid(c)).astype(BF16)
    o_ref[...] = jnp.dot(a, w_ref[...].astype(BF16), preferred_element_type=F32) + b_ref[...]


def _ada(c_all, w_ada, b_ada):
    depth, d, n = w_ada.shape
    r = c_all.shape[0]
    tn = _tile(n, 1024, LANES)
    return pl.pallas_call(
        _ada_kernel,
        out_shape=jax.ShapeDtypeStruct((depth, r, n), F32),
        grid=(depth, n // tn),
        in_specs=[
            pl.BlockSpec((r, d), lambda l, j: (0, 0)),
            pl.BlockSpec((None, d, tn), lambda l, j: (l, 0, j)),
            pl.BlockSpec((None, 1, tn), lambda l, j: (l, 0, j)),
        ],
        out_specs=pl.BlockSpec((None, r, tn), lambda l, j: (l, 0, j)),
        compiler_params=_cparams(("parallel", "parallel")),
        name="ada_mod",
    )(c_all, w_ada, b_ada.reshape(depth, 1, n))


def _ln_mod_kernel(s_ref, mod_ref, h_ref, *, shift_row, scale_row):
    y = _ln(s_ref[...])
    scale = mod_ref[scale_row:scale_row + 1, :]
    shift = mod_ref[shift_row:shift_row + 1, :]
    h_ref[...] = (y * (1.0 + scale) + shift).astype(h_ref.dtype)


def _ln_mod(s_all, mod, ctx_len, shift_row, scale_row):
    b, s, d = s_all.shape
    tr = _tile(ctx_len, 256)
    nct = ctx_len // tr
    return pl.pallas_call(
        functools.partial(_ln_mod_kernel, shift_row=shift_row, scale_row=scale_row),
        out_shape=jax.ShapeDtypeStruct((b, s, d), BF16),
        grid=(b, s // tr),
        in_specs=[
            pl.BlockSpec((None, tr, d), lambda bi, i: (bi, i, 0)),
            pl.BlockSpec((None, None, N_MOD, d), lambda bi, i: (bi, jnp.minimum(i // nct, 1), 0, 0)),
        ],
        out_specs=pl.BlockSpec((None, tr, d), lambda bi, i: (bi, i, 0)),
        compiler_params=_cparams(("parallel", "parallel")),
        name="ln_mod",
    )(s_all, mod)


def _mm_kernel(a_ref, w_ref, b_ref, o_ref):
    acc = jnp.dot(a_ref[...], w_ref[...], preferred_element_type=F32)
    o_ref[...] = (acc + b_ref[...]).astype(o_ref.dtype)


def _matmul(a, w, layer, bias, out_dtype, name):
    m, k = a.shape
    n = w.shape[2]
    tm = _tile(m, 2048)
    tn = _tile(n, 1024, LANES)
    return pl.pallas_call(
        _mm_kernel,
        out_shape=jax.ShapeDtypeStruct((m, n), out_dtype),
        grid=(n // tn, m // tm),
        in_specs=[
            pl.BlockSpec((tm, k), lambda j, i: (i, 0)),
            pl.BlockSpec((None, k, tn), lambda j, i: (layer, 0, j)),
            pl.BlockSpec((1, tn), lambda j, i: (0, j)),
        ],
        out_specs=pl.BlockSpec((tm, tn), lambda j, i: (i, j)),
        compiler_params=_cparams(("parallel", "parallel")),
        name=name,
    )(a, w, bias)


_CONV_ROWS = 128
_CONV_WIN = 256


def _conv_kernel(x_ref, w_ref, b_ref, o_ref, *, segs, kw):
    half = kw // 2
    rb = _CONV_ROWS
    shift_mats = {}

    taps = [k for k in range(kw) if k != half]

    def shifts(win, off):
        if (win, off) not in shift_mats:
            t = lax.broadcasted_iota(I32, (rb, win), 0)
            j = lax.broadcasted_iota(I32, (rb, win), 1)
            shift_mats[(win, off)] = jnp.concatenate(
                [jnp.where(j == t + (off + k - half), 1.0, 0.0).astype(BF16) for k in taps], axis=0)
        return shift_mats[(win, off)]

    for st, ln in segs:
        win = min(_CONV_WIN, ln)
        for r0 in range(st, st + ln, rb):
            lo = min(max(r0 - (win - rb) // 2, st), st + ln - win)
            shifted = jnp.dot(shifts(win, r0 - lo), x_ref[lo:lo + win, :], preferred_element_type=F32)
            acc = b_ref[...] + w_ref[half:half + 1, :] * x_ref[r0:r0 + rb, :].astype(F32)
            for n, k in enumerate(taps):
                acc = acc + w_ref[k:k + 1, :] * shifted[n * rb:(n + 1) * rb, :]
            o_ref[r0:r0 + rb, :] = (acc * _sigmoid(acc)).astype(o_ref.dtype)


def _conv_silu(p_main, conv_w, conv_b, col_off, segs):
    b, s, _ = p_main.shape
    kw, xbc = conv_w.shape
    cw = 512 if (xbc % 512 == 0 and col_off % 512 == 0) else LANES
    assert xbc % cw == 0 and col_off % cw == 0
    assert all(ln % _CONV_ROWS == 0 and (ln >= _CONV_WIN or ln == _CONV_ROWS) for _, ln in segs)
    assert kw // 2 <= (_CONV_WIN - _CONV_ROWS) // 2
    ob = col_off // cw
    return pl.pallas_call(
        functools.partial(_conv_kernel, segs=segs, kw=kw),
        out_shape=jax.ShapeDtypeStruct((b, s, xbc), BF16),
        grid=(b, xbc // cw),
        in_specs=[
            pl.BlockSpec((None, s, cw), lambda bi, j: (bi, 0, ob + j)),
            pl.BlockSpec((kw, cw), lambda bi, j: (0, j)),
            pl.BlockSpec((1, cw), lambda bi, j: (0, j)),
        ],
        out_specs=pl.BlockSpec((None, s, cw), lambda bi, j: (bi, 0, j)),
        compiler_params=_cparams(("parallel", "parallel")),
        name="conv_silu",
    )(p_main, conv_w, conv_b.reshape(1, xbc))


def _head_rows(v_t, c0, hpg, p):
    q = v_t.shape[1]
    return jnp.concatenate(
        [jnp.broadcast_to(v_t[c0 + h:c0 + h + 1, :], (p, q)) for h in range(hpg)], axis=0)


def _ssd_dir(xa_ref, dtr_ref, arow, dtb_row, yt_ref, st_ref, *, backward, nh, ng, p, n, q):
    hpg = nh // ng
    width = nh * p
    gw = hpg * p
    col0 = nh if backward else 0
    dt = _softplus(dtr_ref[...] + dtb_row)
    da = dt * arow
    ri = lax.broadcasted_iota(I32, (q, q), 0)
    ci = lax.broadcasted_iota(I32, (q, q), 1)
    tri = jnp.where((ri <= ci) if backward else (ri >= ci), 1.0, 0.0).astype(F32)
    cs = jnp.dot(tri, da, preferred_element_type=F32, precision=lax.Precision.HIGHEST)
    tot = cs[0:1, :] if backward else cs[q - 1:q, :]
    cdec = jnp.exp(tot)
    dt_t = dt.T
    cs_t = cs.T
    wts_t = (dt * jnp.exp(tot - cs)).T
    ecs_t = jnp.exp(cs).T
    mask_t = (ci <= ri) if backward else (ci >= ri)
    nt_dims = (((1,), (1,)), ((), ()))
    for g in range(ng):
        x_t = xa_ref[:, g * gw:(g + 1) * gw].T.astype(F32)
        bg = xa_ref[:, width + g * n:width + (g + 1) * n]
        cg = xa_ref[:, width + ng * n + g * n:width + ng * n + (g + 1) * n]
        cb_t = lax.dot_general(bg, cg, nt_dims, preferred_element_type=F32)
        c0 = col0 + g * hpg
        yd = []
        for h in range(hpg):
            c = c0 + h
            seg = cs_t[c:c + 1, :] - cs[:, c:c + 1]
            m_t = (cb_t * jnp.exp(jnp.where(mask_t, seg, MASK_NEG))).astype(BF16)
            xdt_t = (x_t[h * p:(h + 1) * p, :] * dt_t[c:c + 1, :]).astype(BF16)
            yd.append(jnp.dot(xdt_t, m_t, preferred_element_type=F32))
        st_old = st_ref[g]
        yo_t = lax.dot_general(st_old.astype(BF16), cg, nt_dims, preferred_element_type=F32)
        y_t = jnp.concatenate(yd, axis=0) + yo_t * _head_rows(ecs_t, c0, hpg, p)
        xw_t = (x_t * _head_rows(wts_t, c0, hpg, p)).astype(BF16)
        cdec_rows = jnp.concatenate(
            [jnp.broadcast_to(cdec[:, c0 + h:c0 + h + 1], (p, n)) for h in range(hpg)], axis=0)
        st_ref[g] = st_old * cdec_rows + jnp.dot(xw_t, bg, preferred_element_type=F32)
        yt_ref[g * gw:(g + 1) * gw, :] = y_t.astype(yt_ref.dtype)


def _ssd_kernel(xf_ref, xb_ref, dtf_ref, dtb_ref, alog_ref, dtbias_ref, yf_ref, yb_ref,
                sf_ref, sb_ref, **kw):
    @pl.when(pl.program_id(1) == 0)
    def _():
        sf_ref[...] = jnp.zeros_like(sf_ref)
        sb_ref[...] = jnp.zeros_like(sb_ref)

    arow = -jnp.exp(alog_ref[...])
    dtb_row = dtbias_ref[...]
    _ssd_dir(xf_ref, dtf_ref, arow, dtb_row, yf_ref, sf_ref, backward=False, **kw)
    _ssd_dir(xb_ref, dtb_ref, arow, dtb_row, yb_ref, sb_ref, backward=True, **kw)


def _ssd(xact, dtraw, alog_row, dtbias_row, nh, p, n, ctx_len):
    b, s, xbc = xact.shape
    q = SSM_CHUNK
    ng = SSM_GROUPS
    width = nh * p
    ncc = ctx_len // q
    nc = s // q
    ncl = nc - ncc

    def fwd(bi, i):
        return (bi, i, 0)

    def bwd_chunk(i):
        return jnp.where(i < ncc, ncc - 1 - i, 2 * ncc + ncl - 1 - i)

    def bwd(bi, i):
        return (bi, bwd_chunk(i), 0)

    def fwd_t(bi, i):
        return (bi, 0, i)

    def bwd_t(bi, i):
        return (bi, 0, bwd_chunk(i))

    kern = functools.partial(_ssd_kernel, nh=nh, ng=ng, p=p, n=n, q=q)
    return pl.pallas_call(
        kern,
        out_shape=(jax.ShapeDtypeStruct((b, width, s), BF16), jax.ShapeDtypeStruct((b, width, s), BF16)),
        grid=(b, nc),
        in_specs=[
            pl.BlockSpec((None, q, xbc), fwd),
            pl.BlockSpec((None, q, xbc), bwd),
            pl.BlockSpec((None, q, LANES), fwd),
            pl.BlockSpec((None, q, LANES), bwd),
            pl.BlockSpec((1, LANES), lambda bi, i: (0, 0)),
            pl.BlockSpec((1, LANES), lambda bi, i: (0, 0)),
        ],
        out_specs=(pl.BlockSpec((None, width, q), fwd_t), pl.BlockSpec((None, width, q), bwd_t)),
        scratch_shapes=[pltpu.VMEM((ng, (nh // ng) * p, n), F32), pltpu.VMEM((ng, (nh // ng) * p, n), F32)],
        compiler_params=_cparams(("parallel", "arbitrary")),
        name="ssd_scan",
    )(xact, xact, dtraw, dtraw, alog_row, dtbias_row)


def _gnorm_kernel(yf_ref, yb_ref, x_ref, z_ref, d_ref, w_ref, o_ref, *, ng):
    x = x_ref[...].astype(F32)
    z = z_ref[...].astype(F32)
    y = (yf_ref[...].astype(F32) + yb_ref[...].astype(F32)).T + d_ref[...] * x
    g = y * (z * _sigmoid(z))
    gw = g.shape[-1] // ng
    for k in range(ng):
        gk = g[:, k * gw:(k + 1) * gw]
        ms = jnp.mean(gk * gk, axis=-1, keepdims=True)
        o_ref[:, k * gw:(k + 1) * gw] = (
            gk * lax.rsqrt(ms + LN_EPS) * w_ref[:, k * gw:(k + 1) * gw]).astype(o_ref.dtype)


def _gated_rmsnorm(yf_t, yb_t, xact, p_main3, d_exp, norm_w):
    b, width, s = yf_t.shape
    tr = _tile(s, 256, LANES)
    blk = pl.BlockSpec((None, tr, width), lambda bi, i: (bi, i, 0))
    blk_t = pl.BlockSpec((None, width, tr), lambda bi, i: (bi, 0, i))
    row = pl.BlockSpec((1, width), lambda bi, i: (0, 0))
    return pl.pallas_call(
        functools.partial(_gnorm_kernel, ng=SSM_GROUPS),
        out_shape=jax.ShapeDtypeStruct((b, s, width), BF16),
        grid=(b, s // tr),
        in_specs=[blk_t, blk_t, blk, blk, row, row],
        out_specs=blk,
        compiler_params=_cparams(("parallel", "parallel")),
        name="gated_rmsnorm",
    )(yf_t, yb_t, xact, p_main3, d_exp, norm_w)


def _gmlp_kernel(u_ref, v_ref, lng_ref, lnb_ref, ws_ref, bst_ref, o_ref, *, ng):
    v = _gelu_tanh(v_ref[...].astype(F32))
    vb = (_ln(v) * lng_ref[...] + lnb_ref[...]).astype(BF16)
    u = _gelu_tanh(u_ref[...].astype(F32))
    gw = u.shape[-1] // ng
    for g in range(ng):
        mixed = jnp.dot(ws_ref[g], vb[:, g * gw:(g + 1) * gw], preferred_element_type=F32)
        mixed = mixed + bst_ref[:, g:g + 1]
        o_ref[:, g * gw:(g + 1) * gw] = (u[:, g * gw:(g + 1) * gw] * mixed).astype(o_ref.dtype)


def _gmlp(p_main, off_u, off_v, ln_g, ln_b, w_s, layer, b_s):
    b, s, _ = p_main.shape
    _, ng, q, _ = w_s.shape
    wg = ln_g.shape[-1]
    assert off_u % wg == 0 and off_v % wg == 0
    bu, bv = off_u // wg, off_v // wg
    row = pl.BlockSpec((1, wg), lambda bi, c: (0, 0))
    return pl.pallas_call(
        functools.partial(_gmlp_kernel, ng=ng),
        out_shape=jax.ShapeDtypeStruct((b, s, wg), BF16),
        grid=(b, s // q),
        in_specs=[
            pl.BlockSpec((None, q, wg), lambda bi, c: (bi, c, bu)),
            pl.BlockSpec((None, q, wg), lambda bi, c: (bi, c, bv)),
            row, row,
            pl.BlockSpec((None, ng, q, q), lambda bi, c: (layer, 0, 0, 0)),
            pl.BlockSpec((q, ng), lambda bi, c: (0, 0)),
        ],
        out_specs=pl.BlockSpec((None, q, wg), lambda bi, c: (bi, c, 0)),
        compiler_params=_cparams(("parallel", "parallel")),
        name="gmlp",
    )(p_main, p_main, ln_g.reshape(1, wg), ln_b.reshape(1, wg), w_s, b_s.T)


def _merge_kernel(a_ref, g_ref, ga_ref, gb_ref, wpa_ref, wpb_ref, o_ref):
    pa = jnp.dot(a_ref[...], wpa_ref[...], preferred_element_type=F32)
    pb = jnp.dot(g_ref[...], wpb_ref[...], preferred_element_type=F32)
    m = _sigmoid(ga_ref[...].astype(F32)) * pa + _sigmoid(gb_ref[...].astype(F32)) * pb
    o_ref[...] = m.astype(o_ref.dtype)


def _merge(ssd_out, g_out, p_main2d, off_ga, off_gb, w_pa, w_pb, layer):
    m, wa = ssd_out.shape
    wb = g_out.shape[1]
    n = w_pa.shape[2]
    tm = _tile(m, 512)
    tn = _tile(n, 1024, LANES)
    assert off_ga % tn == 0 and off_gb % tn == 0
    ba, bb = off_ga // tn, off_gb // tn
    return pl.pallas_call(
        _merge_kernel,
        out_shape=jax.ShapeDtypeStruct((m, n), BF16),
        grid=(n // tn, m // tm),
        in_specs=[
            pl.BlockSpec((tm, wa), lambda j, i: (i, 0)),
            pl.BlockSpec((tm, wb), lambda j, i: (i, 0)),
            pl.BlockSpec((tm, tn), lambda j, i: (i, ba + j)),
            pl.BlockSpec((tm, tn), lambda j, i: (i, bb + j)),
            pl.BlockSpec((None, wa, tn), lambda j, i: (layer, 0, j)),
            pl.BlockSpec((None, wb, tn), lambda j, i: (layer, 0, j)),
        ],
        out_specs=pl.BlockSpec((tm, tn), lambda j, i: (i, j)),
        compiler_params=_cparams(("parallel", "parallel")),
        name="merge_branches",
    )(ssd_out, g_out, p_main2d, p_main2d, w_pa, w_pb)


def _top_k(logits, k):
    lane = lax.broadcasted_iota(I32, logits.shape, 1)
    vals, idxs = [], []
    cur = logits
    for _ in range(k):
        m = jnp.max(cur, axis=-1, keepdims=True)
        idx = jnp.min(jnp.where(cur == m, lane, LANES), axis=-1, keepdims=True)
        vals.append(m)
        idxs.append(idx)
        cur = jnp.where(lane == idx, -jnp.inf, cur)
    return vals, idxs


def _wo_res_kernel(m_ref, wo_ref, s_ref, modc_ref, modl_ref, lng_ref, lnb_ref, wr_ref, br_ref,
                   snew_ref, h_ref, idx_ref, wts_ref, *, alpha, k, sub, nctx):
    y = jnp.dot(m_ref[...], wo_ref[...], preferred_element_type=F32)
    first = pl.program_id(1) == 0
    for r in range(m_ref.shape[0] // sub):
        rows = slice(r * sub, (r + 1) * sub)
        mod = jnp.where(first, modc_ref[...], modl_ref[...]) if r < nctx else modl_ref[...]
        t = alpha * s_ref[rows, :] + mod[2:3, :] * y[rows, :]
        sn = _ln(t) * lng_ref[...] + lnb_ref[...]
        snew_ref[rows, :] = sn
        h = _ln(sn) * (1.0 + mod[4:5, :]) + mod[3:4, :]
        h_ref[rows, :] = _pack_bf16_pair(h)
        logits = jnp.dot(h.astype(BF16), wr_ref[...], preferred_element_type=F32) + br_ref[...]
        vals, idxs = _top_k(logits, k)
        es = [jnp.exp(v - vals[0]) for v in vals]
        den = es[0]
        for e in es[1:]:
            den = den + e
        for j in range(k):
            idx_ref[rows, j:j + 1] = idxs[j]
            wts_ref[rows, j:j + 1] = es[j] / den


_WO_ROWS = 768


def _wo_residual(m_all, w_o, w_router, layer, s_all, mod, ln_g, ln_b, b_router, ctx_len, alpha):
    b, s, d = s_all.shape
    sub = _tile(ctx_len, 256)
    nctx = ctx_len // sub
    tr = sub * max(t for t in range(1, s // sub + 1)
                   if (s // sub) % t == 0 and t >= nctx and t * sub <= max(_WO_ROWS, ctx_len))
    blk = lambda w: pl.BlockSpec((None, tr, w), lambda bi, i: (bi, i, 0))
    row = pl.BlockSpec((1, d), lambda bi, i: (0, 0))
    once = dict(pipeline_mode=pl.Buffered(1))
    return pl.pallas_call(
        functools.partial(_wo_res_kernel, alpha=alpha, k=TOP_K, sub=sub, nctx=nctx),
        out_shape=(
            jax.ShapeDtypeStruct((b, s, d), F32),
            jax.ShapeDtypeStruct((b, s, d // 2), U32),
            jax.ShapeDtypeStruct((b, s, TOP_K), I32),
            jax.ShapeDtypeStruct((b, s, TOP_K), F32),
        ),
        grid=(b, s // tr),
        in_specs=[
            blk(d),
            pl.BlockSpec((None, d, d), lambda bi, i: (layer, 0, 0), **once),
            blk(d),
            pl.BlockSpec((None, None, N_MOD, d), lambda bi, i: (bi, 0, 0, 0)),
            pl.BlockSpec((None, None, N_MOD, d), lambda bi, i: (bi, 1, 0, 0)),
            row, row,
            pl.BlockSpec((None, d, LANES), lambda bi, i: (layer, 0, 0), **once),
            pl.BlockSpec((1, LANES), lambda bi, i: (0, 0)),
        ],
        out_specs=(blk(d), blk(d // 2), blk(TOP_K), blk(TOP_K)),
        compiler_params=_cparams(("parallel", "arbitrary")),
        name="wo_residual_router",
    )(m_all, w_o, s_all, mod, mod, ln_g, ln_b, w_router, b_router)


def _moe_plan(idx, ne, tm):
    m, k = idx.shape
    na = m * k
    assert na % tm == 0
    e = idx.reshape(na)
    onehot = (e[:, None] == jnp.arange(ne, dtype=I32)[None, :]).astype(I32)
    csum = jnp.cumsum(onehot, axis=0)
    rank = jnp.sum(csum * onehot, axis=1) - 1
    counts = csum[-1]
    ptiles = (counts + tm - 1) // tm
    tile_end = jnp.cumsum(ptiles)
    tile_start = tile_end - ptiles
    pos = (jnp.sum(onehot * tile_start[None, :], axis=1) * tm + rank).astype(I32)
    nt = na // tm + ne
    tids = jnp.arange(nt, dtype=I32)
    te = jnp.minimum(jnp.sum((tids[:, None] >= tile_end[None, :]).astype(I32), axis=1), ne - 1)
    return pos, te.astype(I32), tile_end[-1:].astype(I32)


_ISSUE_UNROLL = 8


def _dispatch_kernel(pos_ref, h_ref, xs_in, xs_out, sem, *, k):
    del xs_in
    tr = h_ref.shape[0]

    def issue(i, c):
        for u in range(_ISSUE_UNROLL):
            r = i * _ISSUE_UNROLL + u
            for j in range(k):
                pltpu.make_async_copy(
                    h_ref.at[pl.ds(r, 1)], xs_out.at[pl.ds(pos_ref[0, r * k + j], 1)], sem
                ).start(priority=j % 2)
        return c

    lax.fori_loop(0, tr // _ISSUE_UNROLL, issue, 0)
    for j in range(k):
        pltpu.make_async_copy(h_ref, xs_out.at[pl.ds(0, tr)], sem).wait()


def _dispatch(h_packed, pos, xs):
    m, dh = h_packed.shape
    k = pos.shape[0] // m
    tr = _tile(m, 512, _ISSUE_UNROLL)
    return pl.pallas_call(
        functools.partial(_dispatch_kernel, k=k),
        out_shape=jax.ShapeDtypeStruct(xs.shape, xs.dtype),
        grid=(m // tr,),
        in_specs=[
            pl.BlockSpec((None, 1, tr * k), lambda i: (i, 0, 0), memory_space=pltpu.SMEM),
            pl.BlockSpec((tr, dh), lambda i: (i, 0)),
            pl.BlockSpec(memory_space=pl.ANY),
        ],
        out_specs=pl.BlockSpec(memory_space=pl.ANY),
        scratch_shapes=[pltpu.SemaphoreType.DMA(())],
        input_output_aliases={2: 0},
        compiler_params=_cparams(("arbitrary",)),
        name="moe_dispatch",
    )(pos.reshape(m // tr, 1, tr * k), h_packed, xs)


def _expert_kernel(te_ref, nact_ref, x_ref, wup_ref, bup_ref, wdn_ref, bdn_ref, y_ref, wup_bf, wdn_bf):
    t = pl.program_id(0)
    active = t < nact_ref[0]

    @pl.when(jnp.logical_not(active))
    def _():
        y_ref[...] = jnp.zeros_like(y_ref)

    @pl.when(jnp.logical_and(active, jnp.logical_or(t == 0, te_ref[t] != te_ref[jnp.maximum(t - 1, 0)])))
    def _():
        wup_bf[...] = wup_ref[...].astype(BF16)
        wdn_bf[...] = wdn_ref[...].astype(BF16)

    @pl.when(active)
    def _():
        dh = x_ref.shape[-1]
        lo, hi = _unpack_bf16_pair(x_ref[...])
        hid = jnp.dot(lo.astype(BF16), wup_bf[0:dh, :], preferred_element_type=F32)
        hid = hid + jnp.dot(hi.astype(BF16), wup_bf[dh:, :], preferred_element_type=F32)
        hid = hid + bup_ref[...]
        f = hid.shape[-1] // 2
        glu = jnp.minimum(hid[:, :f], SWIGLU_LIMIT)
        lin = jnp.clip(hid[:, f:], -SWIGLU_LIMIT, SWIGLU_LIMIT)
        act = (glu * _sigmoid(SWIGLU_ALPHA * glu) * (lin + 1.0)).astype(BF16)
        y = jnp.dot(act, wdn_bf[...], preferred_element_type=F32) + bdn_ref[...]
        y_ref[...] = _pack_bf16_pair(y)


def _experts(xs, te, nact, w_up, b_up, w_down, b_down, layer, tm):
    p_rows, dh = xs.shape
    _, ne, d, f2 = w_up.shape
    f = f2 // 2
    nt = p_rows // tm

    def row_map(t, te, nact):
        return (jnp.minimum(t, nact[0] - 1), 0)

    def w_map(t, te, nact):
        return (layer, te[t], 0, 0)

    def b_map(t, te, nact):
        return (te[t], 0, 0)

    grid_spec = pltpu.PrefetchScalarGridSpec(
        num_scalar_prefetch=2,
        grid=(nt,),
        in_specs=[
            pl.BlockSpec((tm, dh), row_map),
            pl.BlockSpec((None, None, d, f2), w_map),
            pl.BlockSpec((None, 1, f2), b_map),
            pl.BlockSpec((None, None, f, d), w_map),
            pl.BlockSpec((None, 1, d), b_map),
        ],
        out_specs=pl.BlockSpec((tm, dh), lambda t, te, nact: (t, 0)),
        scratch_shapes=[pltpu.VMEM((d, f2), BF16), pltpu.VMEM((f, d), BF16)],
    )
    return pl.pallas_call(
        _expert_kernel,
        out_shape=jax.ShapeDtypeStruct((p_rows, dh), U32),
        grid_spec=grid_spec,
        compiler_params=_cparams(("arbitrary",)),
        name="moe_experts",
    )(te, nact, xs, w_up, b_up.reshape(ne, 1, f2), w_down, b_down.reshape(ne, 1, d))


def _combine_kernel(pos_ref, posn_ref, wts_ref, s_ref, mod_ref, modn_ref, lng_ref, lnb_ref, ys_hbm,
                    snew_ref, *rest, alpha, k, with_next):
    if with_next:
        hn_ref, ybuf, sems = rest
    else:
        ybuf, sems = rest
    i = pl.program_id(0)
    n = pl.num_programs(0)
    tr = s_ref.shape[0]
    slot = lax.rem(i, 2)

    def issue(p_ref, sl):
        def body(it, c):
            for u in range(_ISSUE_UNROLL):
                r = it * _ISSUE_UNROLL + u
                for j in range(k):
                    pltpu.make_async_copy(
                        ys_hbm.at[pl.ds(p_ref[0, r * k + j], 1)], ybuf.at[sl, j, pl.ds(r, 1)], sems.at[sl]
                    ).start(priority=j % 2)
            return c

        lax.fori_loop(0, tr // _ISSUE_UNROLL, body, 0)

    @pl.when(i == 0)
    def _():
        issue(pos_ref, 0)

    for sl in range(2):
        @pl.when(jnp.logical_and(i + 1 < n, slot == 1 - sl))
        def _():
            issue(posn_ref, sl)

    for j in range(k):
        pltpu.make_async_copy(ys_hbm.at[pl.ds(0, tr)], ybuf.at[slot, j], sems.at[slot]).wait()

    acc = None
    for j in range(k):
        lo, hi = _unpack_bf16_pair(ybuf[slot, j])
        yk = wts_ref[:, j:j + 1] * jnp.concatenate([lo, hi], axis=1)
        acc = yk if acc is None else acc + yk
    t = alpha * s_ref[...] + mod_ref[5:6, :] * acc
    sn = _ln(t) * lng_ref[...] + lnb_ref[...]
    snew_ref[...] = sn
    if with_next:
        hn_ref[...] = (_ln(sn) * (1.0 + modn_ref[1:2, :]) + modn_ref[0:1, :]).astype(hn_ref.dtype)


def _combine(ys, pos, wts, s2, mod, mod_next, ln_g, ln_b, seg_rows, ctx_len, alpha):
    m, d = s2.shape
    k = wts.shape[1]
    dh = ys.shape[1]
    tr = _tile(ctx_len, 256, _ISSUE_UNROLL)
    nt = m // tr
    spt = seg_rows // tr
    nct = ctx_len // tr
    pos3 = pos.reshape(nt, 1, tr * k)
    with_next = mod_next is not None
    row = pl.BlockSpec((1, d), lambda i: (0, 0))
    rows = pl.BlockSpec((tr, d), lambda i: (i, 0))
    modspec = pl.BlockSpec((None, None, N_MOD, d),
                           lambda i: (i // spt, jnp.minimum((i % spt) // nct, 1), 0, 0))
    out_shape = [jax.ShapeDtypeStruct((m, d), F32)]
    out_specs = [rows]
    if with_next:
        out_shape.append(jax.ShapeDtypeStruct((m, d), BF16))
        out_specs.append(rows)
    return pl.pallas_call(
        functools.partial(_combine_kernel, alpha=alpha, k=k, with_next=with_next),
        out_shape=tuple(out_shape),
        grid=(nt,),
        in_specs=[
            pl.BlockSpec((None, 1, tr * k), lambda i: (i, 0, 0), memory_space=pltpu.SMEM),
            pl.BlockSpec((None, 1, tr * k), lambda i: (jnp.minimum(i + 1, nt - 1), 0, 0),
                         memory_space=pltpu.SMEM),
            pl.BlockSpec((tr, k), lambda i: (i, 0)),
            rows, modspec, modspec, row, row,
            pl.BlockSpec(memory_space=pl.ANY),
        ],
        out_specs=tuple(out_specs),
        scratch_shapes=[pltpu.VMEM((2, k, tr, dh), U32), pltpu.SemaphoreType.DMA((2,))],
        compiler_params=_cparams(("arbitrary",)),
        name="moe_combine_residual",
    )(pos3, pos3, wts, s2, mod, mod_next if with_next else mod, ln_g, ln_b, ys)


def _to_column_major(t, ctx_len):
    b, s, d = t.shape
    lat = t[:, ctx_len:]
    rows = (s - ctx_len) // GRID_W
    lat = lat.reshape(b, rows, GRID_W, d).swapaxes(1, 2).reshape(b, s - ctx_len, d)
    return jnp.concatenate([t[:, :ctx_len], lat], axis=1)


def _to_row_major(t, ctx_len):
    b, s, d = t.shape
    lat = t[:, ctx_len:]
    rows = (s - ctx_len) // GRID_W
    lat = lat.reshape(b, GRID_W, rows, d).swapaxes(1, 2).reshape(b, s - ctx_len, d)
    return jnp.concatenate([t[:, :ctx_len], lat], axis=1)


def kernel(x, c, ctx, c_ctx, w_ada, b_ada, w_in, b_in, conv_w, conv_b, a_log_f, a_log_b, dt_bias_f, dt_bias_b, d_skip, ssm_norm_w, gmlp_ln_g, gmlp_ln_b, w_s, b_s, w_pa, w_pb, w_o, ln_g, ln_b, w_router, b_router, w_up, b_up, w_down, b_down):
    bsz, seq, d = x.shape
    ctx_len = ctx.shape[1]
    depth = w_ada.shape[0]
    s_tot = ctx_len + seq
    m_tot = bsz * s_tot
    nh = a_log_f.shape[1]
    ssm_w = ssm_norm_w.shape[1]
    p = ssm_w // nh
    xbc = conv_w.shape[2]
    n_state = (xbc - ssm_w) // (2 * SSM_GROUPS)
    gw = gmlp_ln_g.shape[1]
    ne = w_router.shape[2]
    assert 2 * nh <= LANES and ne <= LANES

    alpha = (2 * depth) ** 0.25
    off_xbc = ssm_w
    off_dtf = off_xbc + xbc
    off_u = off_dtf + 2 * nh
    off_v = off_u + gw
    off_ga = off_v + gw
    off_gb = off_ga + d
    m_off_u = off_dtf
    m_off_v = m_off_u + gw
    m_off_ga = m_off_v + gw
    m_off_gb = m_off_ga + d

    segs = ((0, ctx_len), (ctx_len, seq))
    tm_moe = 512 if (m_tot * TOP_K) % 512 == 0 else 128

    c_all = jnp.concatenate([c, c_ctx[None, :]], axis=0)
    mods = _ada(c_all, w_ada, b_ada).reshape(depth, bsz + 1, N_MOD, d)

    s_all = jnp.concatenate([ctx, x], axis=1)
    xs = jnp.zeros((m_tot * TOP_K + ne * tm_moe, d // 2), U32)

    def layer_mod(l):
        return jnp.stack(
            [jnp.broadcast_to(mods[l, bsz][None], (bsz, N_MOD, d)), mods[l, :bsz]], axis=1)

    w_main = jnp.concatenate([w_in[:, :, :off_dtf], w_in[:, :, off_u:]], axis=2).astype(BF16)
    w_dt = jnp.pad(w_in[:, :, off_dtf:off_u], ((0, 0), (0, 0), (0, LANES - 2 * nh))).astype(BF16)
    w_pa_b, w_pb_b, w_o_b, w_s_b = (w.astype(BF16) for w in (w_pa, w_pb, w_o, w_s))
    w_r_b = jnp.pad(w_router, ((0, 0), (0, 0), (0, LANES - ne))).astype(BF16)

    h = _ln_mod(s_all, layer_mod(0), ctx_len, 0, 1)
    for l in range(depth):
        column_major = l % 2 == 1
        mod = layer_mod(l)

        b_main = jnp.concatenate([b_in[l][:off_dtf], b_in[l][off_u:]])[None, :]
        b_dt = jnp.pad(b_in[l][off_dtf:off_u], (0, LANES - 2 * nh))[None, :]

        if column_major:
            h = _to_column_major(h, ctx_len)
        h2 = h.reshape(m_tot, d)
        p_main = _matmul(h2, w_main, l, b_main, BF16, "in_proj")
        dtraw = _matmul(h2, w_dt, l, b_dt, F32, "dt_proj").reshape(bsz, s_tot, LANES)
        p3 = p_main.reshape(bsz, s_tot, -1)

        xact = _conv_silu(p3, conv_w[l], conv_b[l], off_xbc, segs)
        alog_row = jnp.pad(jnp.concatenate([a_log_f[l], a_log_b[l]]), (0, LANES - 2 * nh))[None, :]
        dtb_row = jnp.pad(jnp.concatenate([dt_bias_f[l], dt_bias_b[l]]), (0, LANES - 2 * nh))[None, :]
        yf_t, yb_t = _ssd(xact, dtraw, alog_row, dtb_row, nh, p, n_state, ctx_len)
        d_exp = jnp.repeat(d_skip[l], p)[None, :]
        g_out = _gmlp(p3, m_off_u, m_off_v, gmlp_ln_g[l], gmlp_ln_b[l], w_s_b, l, b_s[l])
        ssd_out = _gated_rmsnorm(yf_t, yb_t, xact, p3, d_exp, ssm_norm_w[l][None, :])
        merged = _merge(ssd_out.reshape(m_tot, ssm_w), g_out.reshape(m_tot, gw), p_main,
                        m_off_ga, m_off_gb, w_pa_b, w_pb_b, l)
        merged = merged.reshape(bsz, s_tot, d)
        if column_major:
            merged = _to_row_major(merged, ctx_len)

        b_r = jnp.pad(b_router[l], (0, LANES - ne), constant_values=MASK_NEG)[None, :]
        s_all, h_moe, idx, wts = _wo_residual(
            merged, w_o_b, w_r_b, l, s_all, mod, ln_g[l, 0][None, :], ln_b[l, 0][None, :],
            b_r, ctx_len, alpha)

        pos, te, nact = _moe_plan(idx.reshape(m_tot, TOP_K), ne, tm_moe)
        xs = _dispatch(h_moe.reshape(m_tot, d // 2), pos, xs)
        ys = _experts(xs, te, nact, w_up, b_up[l], w_down, b_down[l], l, tm_moe)
        outs = _combine(ys, pos, wts.reshape(m_tot, TOP_K), s_all.reshape(m_tot, d), mod,
                        layer_mod(l + 1) if l + 1 < depth else None,
                        ln_g[l, 1][None, :], ln_b[l, 1][None, :], s_tot, ctx_len, alpha)
        s_all = outs[0].reshape(bsz, s_tot, d)
        if l + 1 < depth:
            h = outs[1].reshape(bsz, s_tot, d)

    return s_all[:, ctx_len:]
```

```python
import functools

import jax
import jax.numpy as jnp
from jax import lax
from jax.experimental import pallas as pl
from jax.experimental.pallas import tpu as pltpu

F32 = jnp.float32
BF16 = jnp.bfloat16
U32 = jnp.uint32
I32 = jnp.int32

SSM_GROUPS = 8
SSM_CHUNK = 128
TOP_K = 4
GRID_W = 64
N_MOD = 6
SWIGLU_LIMIT = 7.0
SWIGLU_ALPHA = 1.702
LN_EPS = 1e-5

LANES = 128
VMEM_LIMIT = 56 * 1024 * 1024
MASK_NEG = -1e30


def _cparams(sem):
    return pltpu.CompilerParams(dimension_semantics=sem, vmem_limit_bytes=VMEM_LIMIT)


def _tile(n, pref, mult=8):
    if n <= pref:
        return n
    t = (pref // mult) * mult
    while t >= mult:
        if n % t == 0:
            return t
        t -= mult
    raise ValueError(f"no tile for {n} <= {pref}")


def _ln(x):
    mu = jnp.mean(x, axis=-1, keepdims=True)
    xc = x - mu
    var = jnp.mean(xc * xc, axis=-1, keepdims=True)
    return xc * lax.rsqrt(var + LN_EPS)


def _sigmoid(x):
    return 1.0 / (1.0 + jnp.exp(-x))


def _gelu_tanh(x):
    k1 = -2.0 * 0.7978845608028654 * 1.4426950408889634
    k3 = k1 * 0.044715
    return x / (1.0 + jnp.exp2(x * (k1 + k3 * (x * x))))


def _softplus(x):
    return jnp.maximum(x, 0.0) + jnp.log(1.0 + jnp.exp(-jnp.abs(x)))


def _pack_bf16_pair(y):
    dh = y.shape[-1] // 2
    lo = lax.bitcast_convert_type(y[:, :dh].astype(BF16).astype(F32), U32)
    hi = lax.bitcast_convert_type(y[:, dh:].astype(BF16).astype(F32), U32)
    return (hi & jnp.uint32(0xFFFF0000)) | (lo >> 16)


def _unpack_bf16_pair(w):
    lo = lax.bitcast_convert_type(w << 16, F32)
    hi = lax.bitcast_convert_type(w & jnp.uint32(0xFFFF0000), F32)
    return lo, hi


def _ada_kernel(c_ref, w_ref, b_ref, o_ref):
    c = c_ref[...]
    a = (c * _sigmoid(c)).astype(BF16)
    o_ref[...] = jnp.dot(a, w_ref[...].astype(BF16), preferred_element_type=F32) + b_ref[...]


def _ada(c_all, w_ada, b_ada):
    depth, d, n = w_ada.shape
    r = c_all.shape[0]
    tn = _tile(n, 1024, LANES)
    return pl.pallas_call(
        _ada_kernel,
        out_shape=jax.ShapeDtypeStruct((depth, r, n), F32),
        grid=(depth, n // tn),
        in_specs=[
            pl.BlockSpec((r, d), lambda l, j: (0, 0)),
            pl.BlockSpec((None, d, tn), lambda l, j: (l, 0, j)),
            pl.BlockSpec((None, 1, tn), lambda l, j: (l, 0, j)),
        ],
        out_specs=pl.BlockSpec((None, r, tn), lambda l, j: (l, 0, j)),
        compiler_params=_cparams(("parallel", "parallel")),
        name="ada_mod",
    )(c_all, w_ada, b_ada.reshape(depth, 1, n))


def _ln_mod_kernel(s_ref, mod_ref, h_ref, *, shift_row, scale_row):
    y = _ln(s_ref[...])
    scale = mod_ref[scale_row:scale_row + 1, :]
    shift = mod_ref[shift_row:shift_row + 1, :]
    h_ref[...] = (y * (1.0 + scale) + shift).astype(h_ref.dtype)


def _ln_mod(s_all, mod, ctx_len, shift_row, scale_row):
    b, s, d = s_all.shape
    tr = _tile(ctx_len, 256)
    nct = ctx_len // tr
    return pl.pallas_call(
        functools.partial(_ln_mod_kernel, shift_row=shift_row, scale_row=scale_row),
        out_shape=jax.ShapeDtypeStruct((b, s, d), BF16),
        grid=(b, s // tr),
        in_specs=[
            pl.BlockSpec((None, tr, d), lambda bi, i: (bi, i, 0)),
            pl.BlockSpec((None, None, N_MOD, d), lambda bi, i: (bi, jnp.minimum(i // nct, 1), 0, 0)),
        ],
        out_specs=pl.BlockSpec((None, tr, d), lambda bi, i: (bi, i, 0)),
        compiler_params=_cparams(("parallel", "parallel")),
        name="ln_mod",
    )(s_all, mod)


def _mm_kernel(a_ref, w_ref, b_ref, o_ref):
    acc = jnp.dot(a_ref[...], w_ref[...], preferred_element_type=F32)
    o_ref[...] = (acc + b_ref[...]).astype(o_ref.dtype)


def _matmul(a, w, layer, bias, out_dtype, name):
    m, k = a.shape
    n = w.shape[2]
    tm = _tile(m, 2048)
    tn = _tile(n, 1024, LANES)
    return pl.pallas_call(
        _mm_kernel,
        out_shape=jax.ShapeDtypeStruct((m, n), out_dtype),
        grid=(n // tn, m // tm),
        in_specs=[
            pl.BlockSpec((tm, k), lambda j, i: (i, 0)),
            pl.BlockSpec((None, k, tn), lambda j, i: (layer, 0, j)),
            pl.BlockSpec((1, tn), lambda j, i: (0, j)),
        ],
        out_specs=pl.BlockSpec((tm, tn), lambda j, i: (i, j)),
        compiler_params=_cparams(("parallel", "parallel")),
        name=name,
    )(a, w, bias)


_CONV_ROWS = 128
_CONV_WIN = 256


def _conv_kernel(x_ref, w_ref, b_ref, o_ref, *, segs, kw):
    half = kw // 2
    rb = _CONV_ROWS
    shift_mats = {}

    taps = [k for k in range(kw) if k != half]

    def shifts(win, off):
        if (win, off) not in shift_mats:
            t = lax.broadcasted_iota(I32, (rb, win), 0)
            j = lax.broadcasted_iota(I32, (rb, win), 1)
            shift_mats[(win, off)] = jnp.concatenate(
                [jnp.where(j == t + (off + k - half), 1.0, 0.0).astype(BF16) for k in taps], axis=0)
        return shift_mats[(win, off)]

    for st, ln in segs:
        win = min(_CONV_WIN, ln)
        for r0 in range(st, st + ln, rb):
            lo = min(max(r0 - (win - rb) // 2, st), st + ln - win)
            shifted = jnp.dot(shifts(win, r0 - lo), x_ref[lo:lo + win, :], preferred_element_type=F32)
            acc = b_ref[...] + w_ref[half:half + 1, :] * x_ref[r0:r0 + rb, :].astype(F32)
            for n, k in enumerate(taps):
                acc = acc + w_ref[k:k + 1, :] * shifted[n * rb:(n + 1) * rb, :]
            o_ref[r0:r0 + rb, :] = (acc * _sigmoid(acc)).astype(o_ref.dtype)


def _conv_silu(p_main, conv_w, conv_b, col_off, segs):
    b, s, _ = p_main.shape
    kw, xbc = conv_w.shape
    cw = 512 if (xbc % 512 == 0 and col_off % 512 == 0) else LANES
    assert xbc % cw == 0 and col_off % cw == 0
    assert all(ln % _CONV_ROWS == 0 and (ln >= _CONV_WIN or ln == _CONV_ROWS) for _, ln in segs)
    assert kw // 2 <= (_CONV_WIN - _CONV_ROWS) // 2
    ob = col_off // cw
    return pl.pallas_call(
        functools.partial(_conv_kernel, segs=segs, kw=kw),
        out_shape=jax.ShapeDtypeStruct((b, s, xbc), BF16),
        grid=(b, xbc // cw),
        in_specs=[
            pl.BlockSpec((None, s, cw), lambda bi, j: (bi, 0, ob + j)),
            pl.BlockSpec((kw, cw), lambda bi, j: (0, j)),
            pl.BlockSpec((1, cw), lambda bi, j: (0, j)),
        ],
        out_specs=pl.BlockSpec((None, s, cw), lambda bi, j: (bi, 0, j)),
        compiler_params=_cparams(("parallel", "parallel")),
        name="conv_silu",
    )(p_main, conv_w, conv_b.reshape(1, xbc))


def _head_rows(v_t, c0, hpg, p):
    q = v_t.shape[1]
    return jnp.concatenate(
        [jnp.broadcast_to(v_t[c0 + h:c0 + h + 1, :], (p, q)) for h in range(hpg)], axis=0)


def _ssd_dir(xa_ref, dtr_ref, arow, dtb_row, yt_ref, st_ref, *, backward, nh, ng, p, n, q):
    hpg = nh // ng
    width = nh * p
    gw = hpg * p
    col0 = nh if backward else 0
    dt = _softplus(dtr_ref[...] + dtb_row)
    da = dt * arow
    ri = lax.broadcasted_iota(I32, (q, q), 0)
    ci = lax.broadcasted_iota(I32, (q, q), 1)
    tri = jnp.where((ri <= ci) if backward else (ri >= ci), 1.0, 0.0).astype(F32)
    cs = jnp.dot(tri, da, preferred_element_type=F32, precision=lax.Precision.HIGHEST)
    tot = cs[0:1, :] if backward else cs[q - 1:q, :]
    cdec = jnp.exp(tot)
    dt_t = dt.T
    cs_t = cs.T
    wts_t = (dt * jnp.exp(tot - cs)).T
    ecs_t = jnp.exp(cs).T
    mask_t = (ci <= ri) if backward else (ci >= ri)
    nt_dims = (((1,), (1,)), ((), ()))
    for g in range(ng):
        x_t = xa_ref[:, g * gw:(g + 1) * gw].T.astype(F32)
        bg = xa_ref[:, width + g * n:width + (g + 1) * n]
        cg = xa_ref[:, width + ng * n + g * n:width + ng * n + (g + 1) * n]
        cb_t = lax.dot_general(bg, cg, nt_dims, preferred_element_type=F32)
        c0 = col0 + g * hpg
        yd = []
        for h in range(hpg):
            c = c0 + h
            seg = cs_t[c:c + 1, :] - cs[:, c:c + 1]
            m_t = (cb_t * jnp.exp(jnp.where(mask_t, seg, MASK_NEG))).astype(BF16)
            xdt_t = (x_t[h * p:(h + 1) * p, :] * dt_t[c:c + 1, :]).astype(BF16)
            yd.append(jnp.dot(xdt_t, m_t, preferred_element_type=F32))
        st_old = st_ref[g]
        yo_t = lax.dot_general(st_old.astype(BF16), cg, nt_dims, preferred_element_type=F32)
        y_t = jnp.concatenate(yd, axis=0) + yo_t * _head_rows(ecs_t, c0, hpg, p)
        xw_t = (x_t * _head_rows(wts_t, c0, hpg, p)).astype(BF16)
        cdec_rows = jnp.concatenate(
            [jnp.broadcast_to(cdec[:, c0 + h:c0 + h + 1], (p, n)) for h in range(hpg)], axis=0)
        st_ref[g] = st_old * cdec_rows + jnp.dot(xw_t, bg, preferred_element_type=F32)
        yt_ref[g * gw:(g + 1) * gw, :] = y_t.astype(yt_ref.dtype)


def _ssd_kernel(xf_ref, xb_ref, dtf_ref, dtb_ref, alog_ref, dtbias_ref, yf_ref, yb_ref,
                sf_ref, sb_ref, **kw):
    @pl.when(pl.program_id(1) == 0)
    def _():
        sf_ref[...] = jnp.zeros_like(sf_ref)
        sb_ref[...] = jnp.zeros_like(sb_ref)

    arow = -jnp.exp(alog_ref[...])
    dtb_row = dtbias_ref[...]
    _ssd_dir(xf_ref, dtf_ref, arow, dtb_row, yf_ref, sf_ref, backward=False, **kw)
    _ssd_dir(xb_ref, dtb_ref, arow, dtb_row, yb_ref, sb_ref, backward=True, **kw)


def _ssd(xact, dtraw, alog_row, dtbias_row, nh, p, n, ctx_len):
    b, s, xbc = xact.shape
    q = SSM_CHUNK
    ng = SSM_GROUPS
    width = nh * p
    ncc = ctx_len // q
    nc = s // q
    ncl = nc - ncc

    def fwd(bi, i):
        return (bi, i, 0)

    def bwd_chunk(i):
        return jnp.where(i < ncc, ncc - 1 - i, 2 * ncc + ncl - 1 - i)

    def bwd(bi, i):
        return (bi, bwd_chunk(i), 0)

    def fwd_t(bi, i):
        return (bi, 0, i)

    def bwd_t(bi, i):
        return (bi, 0, bwd_chunk(i))

    kern = functools.partial(_ssd_kernel, nh=nh, ng=ng, p=p, n=n, q=q)
    return pl.pallas_call(
        kern,
        out_shape=(jax.ShapeDtypeStruct((b, width, s), BF16), jax.ShapeDtypeStruct((b, width, s), BF16)),
        grid=(b, nc),
        in_specs=[
            pl.BlockSpec((None, q, xbc), fwd),
            pl.BlockSpec((None, q, xbc), bwd),
            pl.BlockSpec((None, q, LANES), fwd),
            pl.BlockSpec((None, q, LANES), bwd),
            pl.BlockSpec((1, LANES), lambda bi, i: (0, 0)),
            pl.BlockSpec((1, LANES), lambda bi, i: (0, 0)),
        ],
        out_specs=(pl.BlockSpec((None, width, q), fwd_t), pl.BlockSpec((None, width, q), bwd_t)),
        scratch_shapes=[pltpu.VMEM((ng, (nh // ng) * p, n), F32), pltpu.VMEM((ng, (nh // ng) * p, n), F32)],
        compiler_params=_cparams(("parallel", "arbitrary")),
        name="ssd_scan",
    )(xact, xact, dtraw, dtraw, alog_row, dtbias_row)


def _gnorm_kernel(yf_ref, yb_ref, x_ref, z_ref, d_ref, w_ref, o_ref, *, ng):
    x = x_ref[...].astype(F32)
    z = z_ref[...].astype(F32)
    y = (yf_ref[...].astype(F32) + yb_ref[...].astype(F32)).T + d_ref[...] * x
    g = y * (z * _sigmoid(z))
    gw = g.shape[-1] // ng
    for k in range(ng):
        gk = g[:, k * gw:(k + 1) * gw]
        ms = jnp.mean(gk * gk, axis=-1, keepdims=True)
        o_ref[:, k * gw:(k + 1) * gw] = (
            gk * lax.rsqrt(ms + LN_EPS) * w_ref[:, k * gw:(k + 1) * gw]).astype(o_ref.dtype)


def _gated_rmsnorm(yf_t, yb_t, xact, p_main3, d_exp, norm_w):
    b, width, s = yf_t.shape
    tr = _tile(s, 256, LANES)
    blk = pl.BlockSpec((None, tr, width), lambda bi, i: (bi, i, 0))
    blk_t = pl.BlockSpec((None, width, tr), lambda bi, i: (bi, 0, i))
    row = pl.BlockSpec((1, width), lambda bi, i: (0, 0))
    return pl.pallas_call(
        functools.partial(_gnorm_kernel, ng=SSM_GROUPS),
        out_shape=jax.ShapeDtypeStruct((b, s, width), BF16),
        grid=(b, s // tr),
        in_specs=[blk_t, blk_t, blk, blk, row, row],
        out_specs=blk,
        compiler_params=_cparams(("parallel", "parallel")),
        name="gated_rmsnorm",
    )(yf_t, yb_t, xact, p_main3, d_exp, norm_w)


def _gmlp_kernel(u_ref, v_ref, lng_ref, lnb_ref, ws_ref, bst_ref, o_ref, *, ng):
    v = _gelu_tanh(v_ref[...].astype(F32))
    vb = (_ln(v) * lng_ref[...] + lnb_ref[...]).astype(BF16)
    u = _gelu_tanh(u_ref[...].astype(F32))
    gw = u.shape[-1] // ng
    for g in range(ng):
        mixed = jnp.dot(ws_ref[g], vb[:, g * gw:(g + 1) * gw], preferred_element_type=F32)
        mixed = mixed + bst_ref[:, g:g + 1]
        o_ref[:, g * gw:(g + 1) * gw] = (u[:, g * gw:(g + 1) * gw] * mixed).astype(o_ref.dtype)


def _gmlp(p_main, off_u, off_v, ln_g, ln_b, w_s, layer, b_s):
    b, s, _ = p_main.shape
    _, ng, q, _ = w_s.shape
    wg = ln_g.shape[-1]
    assert off_u % wg == 0 and off_v % wg == 0
    bu, bv = off_u // wg, off_v // wg
    row = pl.BlockSpec((1, wg), lambda bi, c: (0, 0))
    return pl.pallas_call(
        functools.partial(_gmlp_kernel, ng=ng),
        out_shape=jax.ShapeDtypeStruct((b, s, wg), BF16),
        grid=(b, s // q),
        in_specs=[
            pl.BlockSpec((None, q, wg), lambda bi, c: (bi, c, bu)),
            pl.BlockSpec((None, q, wg), lambda bi, c: (bi, c, bv)),
            row, row,
            pl.BlockSpec((None, ng, q, q), lambda bi, c: (layer, 0, 0, 0)),
            pl.BlockSpec((q, ng), lambda bi, c: (0, 0)),
        ],
        out_specs=pl.BlockSpec((None, q, wg), lambda bi, c: (bi, c, 0)),
        compiler_params=_cparams(("parallel", "parallel")),
        name="gmlp",
    )(p_main, p_main, ln_g.reshape(1, wg), ln_b.reshape(1, wg), w_s, b_s.T)


def _merge_kernel(a_ref, g_ref, ga_ref, gb_ref, wpa_ref, wpb_ref, o_ref):
    pa = jnp.dot(a_ref[...], wpa_ref[...], preferred_element_type=F32)
    pb = jnp.dot(g_ref[...], wpb_ref[...], preferred_element_type=F32)
    m = _sigmoid(ga_ref[...].astype(F32)) * pa + _sigmoid(gb_ref[...].astype(F32)) * pb
    o_ref[...] = m.astype(o_ref.dtype)


def _merge(ssd_out, g_out, p_main2d, off_ga, off_gb, w_pa, w_pb, layer):
    m, wa = ssd_out.shape
    wb = g_out.shape[1]
    n = w_pa.shape[2]
    tm = _tile(m, 512)
    tn = _tile(n, 1024, LANES)
    assert off_ga % tn == 0 and off_gb % tn == 0
    ba, bb = off_ga // tn, off_gb // tn
    return pl.pallas_call(
        _merge_kernel,
        out_shape=jax.ShapeDtypeStruct((m, n), BF16),
        grid=(n // tn, m // tm),
        in_specs=[
            pl.BlockSpec((tm, wa), lambda j, i: (i, 0)),
            pl.BlockSpec((tm, wb), lambda j, i: (i, 0)),
            pl.BlockSpec((tm, tn), lambda j, i: (i, ba + j)),
            pl.BlockSpec((tm, tn), lambda j, i: (i, bb + j)),
            pl.BlockSpec((None, wa, tn), lambda j, i: (layer, 0, j)),
            pl.BlockSpec((None, wb, tn), lambda j, i: (layer, 0, j)),
        ],
        out_specs=pl.BlockSpec((tm, tn), lambda j, i: (i, j)),
        compiler_params=_cparams(("parallel", "parallel")),
        name="merge_branches",
    )(ssd_out, g_out, p_main2d, p_main2d, w_pa, w_pb)


def _top_k(logits, k):
    lane = lax.broadcasted_iota(I32, logits.shape, 1)
    vals, idxs = [], []
    cur = logits
    for _ in range(k):
        m = jnp.max(cur, axis=-1, keepdims=True)
        idx = jnp.min(jnp.where(cur == m, lane, LANES), axis=-1, keepdims=True)
        vals.append(m)
        idxs.append(idx)
        cur = jnp.where(lane == idx, -jnp.inf, cur)
    return vals, idxs


def _wo_res_kernel(m_ref, wo_ref, s_ref, modc_ref, modl_ref, lng_ref, lnb_ref, wr_ref, br_ref,
                   snew_ref, h_ref, idx_ref, wts_ref, *, alpha, k, sub, nctx):
    y = jnp.dot(m_ref[...], wo_ref[...], preferred_element_type=F32)
    first = pl.program_id(1) == 0
    for r in range(m_ref.shape[0] // sub):
        rows = slice(r * sub, (r + 1) * sub)
        mod = jnp.where(first, modc_ref[...], modl_ref[...]) if r < nctx else modl_ref[...]
        t = alpha * s_ref[rows, :] + mod[2:3, :] * y[rows, :]
        sn = _ln(t) * lng_ref[...] + lnb_ref[...]
        snew_ref[rows, :] = sn
        h = _ln(sn) * (1.0 + mod[4:5, :]) + mod[3:4, :]
        h_ref[rows, :] = _pack_bf16_pair(h)
        logits = jnp.dot(h.astype(BF16), wr_ref[...], preferred_element_type=F32) + br_ref[...]
        vals, idxs = _top_k(logits, k)
        es = [jnp.exp(v - vals[0]) for v in vals]
        den = es[0]
        for e in es[1:]:
            den = den + e
        for j in range(k):
            idx_ref[rows, j:j + 1] = idxs[j]
            wts_ref[rows, j:j + 1] = es[j] / den


_WO_ROWS = 768


def _wo_residual(m_all, w_o, w_router, layer, s_all, mod, ln_g, ln_b, b_router, ctx_len, alpha):
    b, s, d = s_all.shape
    sub = _tile(ctx_len, 256)
    nctx = ctx_len // sub
    tr = sub * max(t for t in range(1, s // sub + 1)
                   if (s // sub) % t == 0 and t >= nctx and t * sub <= max(_WO_ROWS, ctx_len))
    blk = lambda w: pl.BlockSpec((None, tr, w), lambda bi, i: (bi, i, 0))
    row = pl.BlockSpec((1, d), lambda bi, i: (0, 0))
    once = dict(pipeline_mode=pl.Buffered(1))
    return pl.pallas_call(
        functools.partial(_wo_res_kernel, alpha=alpha, k=TOP_K, sub=sub, nctx=nctx),
        out_shape=(
            jax.ShapeDtypeStruct((b, s, d), F32),
            jax.ShapeDtypeStruct((b, s, d // 2), U32),
            jax.ShapeDtypeStruct((b, s, TOP_K), I32),
            jax.ShapeDtypeStruct((b, s, TOP_K), F32),
        ),
        grid=(b, s // tr),
        in_specs=[
            blk(d),
            pl.BlockSpec((None, d, d), lambda bi, i: (layer, 0, 0), **once),
            blk(d),
            pl.BlockSpec((None, None, N_MOD, d), lambda bi, i: (bi, 0, 0, 0)),
            pl.BlockSpec((None, None, N_MOD, d), lambda bi, i: (bi, 1, 0, 0)),
            row, row,
            pl.BlockSpec((None, d, LANES), lambda bi, i: (layer, 0, 0), **once),
            pl.BlockSpec((1, LANES), lambda bi, i: (0, 0)),
        ],
        out_specs=(blk(d), blk(d // 2), blk(TOP_K), blk(TOP_K)),
        compiler_params=_cparams(("parallel", "arbitrary")),
        name="wo_residual_router",
    )(m_all, w_o, s_all, mod, mod, ln_g, ln_b, w_router, b_router)


def _moe_plan(idx, ne, tm):
    m, k = idx.shape
    na = m * k
    assert na % tm == 0
    e = idx.reshape(na)
    onehot = (e[:, None] == jnp.arange(ne, dtype=I32)[None, :]).astype(I32)
    csum = jnp.cumsum(onehot, axis=0)
    rank = jnp.sum(csum * onehot, axis=1) - 1
    counts = csum[-1]
    ptiles = (counts + tm - 1) // tm
    tile_end = jnp.cumsum(ptiles)
    tile_start = tile_end - ptiles
    pos = (jnp.sum(onehot * tile_start[None, :], axis=1) * tm + rank).astype(I32)
    nt = na // tm + ne
    tids = jnp.arange(nt, dtype=I32)
    te = jnp.minimum(jnp.sum((tids[:, None] >= tile_end[None, :]).astype(I32), axis=1), ne - 1)
    return pos, te.astype(I32), tile_end[-1:].astype(I32)


_ISSUE_UNROLL = 8


def _dispatch_kernel(pos_ref, h_ref, xs_in, xs_out, sem, *, k):
    del xs_in
    tr = h_ref.shape[0]

    def issue(i, c):
        for u in range(_ISSUE_UNROLL):
            r = i * _ISSUE_UNROLL + u
            for j in range(k):
                pltpu.make_async_copy(
                    h_ref.at[pl.ds(r, 1)], xs_out.at[pl.ds(pos_ref[0, r * k + j], 1)], sem
                ).start(priority=j % 2)
        return c

    lax.fori_loop(0, tr // _ISSUE_UNROLL, issue, 0)
    for j in range(k):
        pltpu.make_async_copy(h_ref, xs_out.at[pl.ds(0, tr)], sem).wait()


def _dispatch(h_packed, pos, xs):
    m, dh = h_packed.shape
    k = pos.shape[0] // m
    tr = _tile(m, 512, _ISSUE_UNROLL)
    return pl.pallas_call(
        functools.partial(_dispatch_kernel, k=k),
        out_shape=jax.ShapeDtypeStruct(xs.shape, xs.dtype),
        grid=(m // tr,),
        in_specs=[
            pl.BlockSpec((None, 1, tr * k), lambda i: (i, 0, 0), memory_space=pltpu.SMEM),
            pl.BlockSpec((tr, dh), lambda i: (i, 0)),
            pl.BlockSpec(memory_space=pl.ANY),
        ],
        out_specs=pl.BlockSpec(memory_space=pl.ANY),
        scratch_shapes=[pltpu.SemaphoreType.DMA(())],
        input_output_aliases={2: 0},
        compiler_params=_cparams(("arbitrary",)),
        name="moe_dispatch",
    )(pos.reshape(m // tr, 1, tr * k), h_packed, xs)


def _expert_kernel(te_ref, nact_ref, x_ref, wup_ref, bup_ref, wdn_ref, bdn_ref, y_ref, wup_bf, wdn_bf):
    t = pl.program_id(0)
    active = t < nact_ref[0]

    @pl.when(jnp.logical_not(active))
    def _():
        y_ref[...] = jnp.zeros_like(y_ref)

    @pl.when(jnp.logical_and(active, jnp.logical_or(t == 0, te_ref[t] != te_ref[jnp.maximum(t - 1, 0)])))
    def _():
        wup_bf[...] = wup_ref[...].astype(BF16)
        wdn_bf[...] = wdn_ref[...].astype(BF16)

    @pl.when(active)
    def _():
        dh = x_ref.shape[-1]
        lo, hi = _unpack_bf16_pair(x_ref[...])
        hid = jnp.dot(lo.astype(BF16), wup_bf[0:dh, :], preferred_element_type=F32)
        hid = hid + jnp.dot(hi.astype(BF16), wup_bf[dh:, :], preferred_element_type=F32)
        hid = hid + bup_ref[...]
        f = hid.shape[-1] // 2
        glu = jnp.minimum(hid[:, :f], SWIGLU_LIMIT)
        lin = jnp.clip(hid[:, f:], -SWIGLU_LIMIT, SWIGLU_LIMIT)
        act = (glu * _sigmoid(SWIGLU_ALPHA * glu) * (lin + 1.0)).astype(BF16)
        y = jnp.dot(act, wdn_bf[...], preferred_element_type=F32) + bdn_ref[...]
        y_ref[...] = _pack_bf16_pair(y)


def _experts(xs, te, nact, w_up, b_up, w_down, b_down, layer, tm):
    p_rows, dh = xs.shape
    _, ne, d, f2 = w_up.shape
    f = f2 // 2
    nt = p_rows // tm

    def row_map(t, te, nact):
        return (jnp.minimum(t, nact[0] - 1), 0)

    def w_map(t, te, nact):
        return (layer, te[t], 0, 0)

    def b_map(t, te, nact):
        return (te[t], 0, 0)

    grid_spec = pltpu.PrefetchScalarGridSpec(
        num_scalar_prefetch=2,
        grid=(nt,),
        in_specs=[
            pl.BlockSpec((tm, dh), row_map),
            pl.BlockSpec((None, None, d, f2), w_map),
            pl.BlockSpec((None, 1, f2), b_map),
            pl.BlockSpec((None, None, f, d), w_map),
            pl.BlockSpec((None, 1, d), b_map),
        ],
        out_specs=pl.BlockSpec((tm, dh), lambda t, te, nact: (t, 0)),
        scratch_shapes=[pltpu.VMEM((d, f2), BF16), pltpu.VMEM((f, d), BF16)],
    )
    return pl.pallas_call(
        _expert_kernel,
        out_shape=jax.ShapeDtypeStruct((p_rows, dh), U32),
        grid_spec=grid_spec,
        compiler_params=_cparams(("arbitrary",)),
        name="moe_experts",
    )(te, nact, xs, w_up, b_up.reshape(ne, 1, f2), w_down, b_down.reshape(ne, 1, d))


def _combine_kernel(pos_ref, posn_ref, wts_ref, s_ref, mod_ref, modn_ref, lng_ref, lnb_ref, ys_hbm,
                    snew_ref, *rest, alpha, k, with_next):
    if with_next:
        hn_ref, ybuf, sems = rest
    else:
        ybuf, sems = rest
    i = pl.program_id(0)
    n = pl.num_programs(0)
    tr = s_ref.shape[0]
    slot = lax.rem(i, 2)

    def issue(p_ref, sl):
        def body(it, c):
            for u in range(_ISSUE_UNROLL):
                r = it * _ISSUE_UNROLL + u
                for j in range(k):
                    pltpu.make_async_copy(
                        ys_hbm.at[pl.ds(p_ref[0, r * k + j], 1)], ybuf.at[sl, j, pl.ds(r, 1)], sems.at[sl]
                    ).start(priority=j % 2)
            return c

        lax.fori_loop(0, tr // _ISSUE_UNROLL, body, 0)

    @pl.when(i == 0)
    def _():
        issue(pos_ref, 0)

    for sl in range(2):
        @pl.when(jnp.logical_and(i + 1 < n, slot == 1 - sl))
        def _():
            issue(posn_ref, sl)

    for j in range(k):
        pltpu.make_async_copy(ys_hbm.at[pl.ds(0, tr)], ybuf.at[slot, j], sems.at[slot]).wait()

    acc = None
    for j in range(k):
        lo, hi = _unpack_bf16_pair(ybuf[slot, j])
        yk = wts_ref[:, j:j + 1] * jnp.concatenate([lo, hi], axis=1)
        acc = yk if acc is None else acc + yk
    t = alpha * s_ref[...] + mod_ref[5:6, :] * acc
    sn = _ln(t) * lng_ref[...] + lnb_ref[...]
    snew_ref[...] = sn
    if with_next:
        hn_ref[...] = (_ln(sn) * (1.0 + modn_ref[1:2, :]) + modn_ref[0:1, :]).astype(hn_ref.dtype)


def _combine(ys, pos, wts, s2, mod, mod_next, ln_g, ln_b, seg_rows, ctx_len, alpha):
    m, d = s2.shape
    k = wts.shape[1]
    dh = ys.shape[1]
    tr = _tile(ctx_len, 256, _ISSUE_UNROLL)
    nt = m // tr
    spt = seg_rows // tr
    nct = ctx_len // tr
    pos3 = pos.reshape(nt, 1, tr * k)
    with_next = mod_next is not None
    row = pl.BlockSpec((1, d), lambda i: (0, 0))
    rows = pl.BlockSpec((tr, d), lambda i: (i, 0))
    modspec = pl.BlockSpec((None, None, N_MOD, d),
                           lambda i: (i // spt, jnp.minimum((i % spt) // nct, 1), 0, 0))
    out_shape = [jax.ShapeDtypeStruct((m, d), F32)]
    out_specs = [rows]
    if with_next:
        out_shape.append(jax.ShapeDtypeStruct((m, d), BF16))
        out_specs.append(rows)
    return pl.pallas_call(
        functools.partial(_combine_kernel, alpha=alpha, k=k, with_next=with_next),
        out_shape=tuple(out_shape),
        grid=(nt,),
        in_specs=[
            pl.BlockSpec((None, 1, tr * k), lambda i: (i, 0, 0), memory_space=pltpu.SMEM),
            pl.BlockSpec((None, 1, tr * k), lambda i: (jnp.minimum(i + 1, nt - 1), 0, 0),
                         memory_space=pltpu.SMEM),
            pl.BlockSpec((tr, k), lambda i: (i, 0)),
            rows, modspec, modspec, row, row,
            pl.BlockSpec(memory_space=pl.ANY),
        ],
        out_specs=tuple(out_specs),
        scratch_shapes=[pltpu.VMEM((2, k, tr, dh), U32), pltpu.SemaphoreType.DMA((2,))],
        compiler_params=_cparams(("arbitrary",)),
        name="moe_combine_residual",
    )(pos3, pos3, wts, s2, mod, mod_next if with_next else mod, ln_g, ln_b, ys)


def _to_column_major(t, ctx_len):
    b, s, d = t.shape
    lat = t[:, ctx_len:]
    rows = (s - ctx_len) // GRID_W
    lat = lat.reshape(b, rows, GRID_W, d).swapaxes(1, 2).reshape(b, s - ctx_len, d)
    return jnp.concatenate([t[:, :ctx_len], lat], axis=1)


def _to_row_major(t, ctx_len):
    b, s, d = t.shape
    lat = t[:, ctx_len:]
    rows = (s - ctx_len) // GRID_W
    lat = lat.reshape(b, GRID_W, rows, d).swapaxes(1, 2).reshape(b, s - ctx_len, d)
    return jnp.concatenate([t[:, :ctx_len], lat], axis=1)


def kernel(x, c, ctx, c_ctx, w_ada, b_ada, w_in, b_in, conv_w, conv_b, a_log_f, a_log_b, dt_bias_f, dt_bias_b, d_skip, ssm_norm_w, gmlp_ln_g, gmlp_ln_b, w_s, b_s, w_pa, w_pb, w_o, ln_g, ln_b, w_router, b_router, w_up, b_up, w_down, b_down):
    bsz, seq, d = x.shape
    ctx_len = ctx.shape[1]
    depth = w_ada.shape[0]
    s_tot = ctx_len + seq
    m_tot = bsz * s_tot
    nh = a_log_f.shape[1]
    ssm_w = ssm_norm_w.shape[1]
    p = ssm_w // nh
    xbc = conv_w.shape[2]
    n_state = (xbc - ssm_w) // (2 * SSM_GROUPS)
    gw = gmlp_ln_g.shape[1]
    ne = w_router.shape[2]
    assert 2 * nh <= LANES and ne <= LANES

    alpha = (2 * depth) ** 0.25
    off_xbc = ssm_w
    off_dtf = off_xbc + xbc
    off_u = off_dtf + 2 * nh
    off_v = off_u + gw
    off_ga = off_v + gw
    off_gb = off_ga + d
    m_off_u = off_dtf
    m_off_v = m_off_u + gw
    m_off_ga = m_off_v + gw
    m_off_gb = m_off_ga + d

    segs = ((0, ctx_len), (ctx_len, seq))
    tm_moe = 512 if (m_tot * TOP_K) % 512 == 0 else 128

    c_all = jnp.concatenate([c, c_ctx[None, :]], axis=0)
    mods = _ada(c_all, w_ada, b_ada).reshape(depth, bsz + 1, N_MOD, d)

    s_all = jnp.concatenate([ctx, x], axis=1)
    xs = jnp.zeros((m_tot * TOP_K + ne * tm_moe, d // 2), U32)

    def layer_mod(l):
        return jnp.stack(
            [jnp.broadcast_to(mods[l, bsz][None], (bsz, N_MOD, d)), mods[l, :bsz]], axis=1)

    w_main = jnp.concatenate([w_in[:, :, :off_dtf], w_in[:, :, off_u:]], axis=2).astype(BF16)
    w_dt = jnp.pad(w_in[:, :, off_dtf:off_u], ((0, 0), (0, 0), (0, LANES - 2 * nh))).astype(BF16)
    w_pa_b, w_pb_b, w_o_b, w_s_b = (w.astype(BF16) for w in (w_pa, w_pb, w_o, w_s))
    w_r_b = jnp.pad(w_router, ((0, 0), (0, 0), (0, LANES - ne))).astype(BF16)

    h = _ln_mod(s_all, layer_mod(0), ctx_len, 0, 1)
    for l in range(depth):
        column_major = l % 2 == 1
        mod = layer_mod(l)

        b_main = jnp.concatenate([b_in[l][:off_dtf], b_in[l][off_u:]])[None, :]
        b_dt = jnp.pad(b_in[l][off_dtf:off_u], (0, LANES - 2 * nh))[None, :]

        if column_major:
            h = _to_column_major(h, ctx_len)
        h2 = h.reshape(m_tot, d)
        p_main = _matmul(h2, w_main, l, b_main, BF16, "in_proj")
        dtraw = _matmul(h2, w_dt, l, b_dt, F32, "dt_proj").reshape(bsz, s_tot, LANES)
        p3 = p_main.reshape(bsz, s_tot, -1)

        xact = _conv_silu(p3, conv_w[l], conv_b[l], off_xbc, segs)
        alog_row = jnp.pad(jnp.concatenate([a_log_f[l], a_log_b[l]]), (0, LANES - 2 * nh))[None, :]
        dtb_row = jnp.pad(jnp.concatenate([dt_bias_f[l], dt_bias_b[l]]), (0, LANES - 2 * nh))[None, :]
        yf_t, yb_t = _ssd(xact, dtraw, alog_row, dtb_row, nh, p, n_state, ctx_len)
        d_exp = jnp.repeat(d_skip[l], p)[None, :]
        g_out = _gmlp(p3, m_off_u, m_off_v, gmlp_ln_g[l], gmlp_ln_b[l], w_s_b, l, b_s[l])
        ssd_out = _gated_rmsnorm(yf_t, yb_t, xact, p3, d_exp, ssm_norm_w[l][None, :])
        merged = _merge(ssd_out.reshape(m_tot, ssm_w), g_out.reshape(m_tot, gw), p_main,
                        m_off_ga, m_off_gb, w_pa_b, w_pb_b, l)
        merged = merged.reshape(bsz, s_tot, d)
        if column_major:
            merged = _to_row_major(merged, ctx_len)

        b_r = jnp.pad(b_router[l], (0, LANES - ne), constant_values=MASK_NEG)[None, :]
        s_all, h_moe, idx, wts = _wo_residual(
            merged, w_o_b, w_r_b, l, s_all, mod, ln_g[l, 0][None, :], ln_b[l, 0][None, :],
            b_r, ctx_len, alpha)

        pos, te, nact = _moe_plan(idx.reshape(m_tot, TOP_K), ne, tm_moe)
        xs = _dispatch(h_moe.reshape(m_tot, d // 2), pos, xs)
        ys = _experts(xs, te, nact, w_up, b_up[l], w_down, b_down[l], l, tm_moe)
        outs = _combine(ys, pos, wts.reshape(m_tot, TOP_K), s_all.reshape(m_tot, d), mod,
                        layer_mod(l + 1) if l + 1 < depth else None,
                        ln_g[l, 1][None, :], ln_b[l, 1][None, :], s_tot, ctx_len, alpha)
        s_all = outs[0].reshape(bsz, s_tot, d)
        if l + 1 < depth:
            h = outs[1].reshape(bsz, s_tot, d)

    return s_all[:, ctx_len:]
```
